```python
import math
import jax, jax.numpy as jnp
from jax import lax
import numpy as np

D_MODEL = 1024
BATCH = 4
SEQ = 4096
DEPTH = 4
DEC_BATCH = 8
DEC_SEQ = 16
PAST_LEN = 4096

CHUNK = 64
Q_BLOCK = 128
N_BRANCH = 4
A_HEADS = 4
A_NOPE = 64
A_ROPE = 32
A_V = 64
A_Q_LORA = 192
A_KV_LORA = 128
A_SCALE = (A_NOPE + A_ROPE) ** -0.5
ROPE_BASE = 10000.0
B_HEADS = 4
B_DIM = 64
B_SCALE = B_DIM ** -0.5
IDX_HEADS = 4
IDX_DIM = 32
IDX_SCALE = (IDX_HEADS ** -0.5) * (IDX_DIM ** -0.5)
TOPK_MAX = 256
REL_BUCKETS = 32
REL_MAX_DIST = 128
POOL_WINDOWS = (2, 4, 8, 16)
POOL_WIDTH = 256
POOL_GROUP = POOL_WIDTH // len(POOL_WINDOWS)
POOL_STATE = max(POOL_WINDOWS) - 1
CONV_WIDTH = 256
CONV_K = 31
D_FF = 2816
FFN_K = 3
ALPHA = (2 * DEPTH) ** 0.25
BETA = (8 * DEPTH) ** -0.25
LN_EPS = 1e-5
F32 = jnp.float32

IN_SPLITS = (A_Q_LORA, A_KV_LORA, A_ROPE,
             B_HEADS * B_DIM, B_HEADS * B_DIM, B_HEADS * B_DIM,
             IDX_HEADS * IDX_DIM, IDX_DIM, IDX_HEADS,
             POOL_WIDTH, 2 * CONV_WIDTH, N_BRANCH * D_MODEL)
D_IN = sum(IN_SPLITS)
IN_OFFSETS = tuple(int(v) for v in np.cumsum(IN_SPLITS)[:-1])

kernel_name = 'hybrid_streaming_encoder_step'


def layer_norm(x, g, b):
    xf = x.astype(F32)
    mu = jnp.mean(xf, axis=-1, keepdims=True)
    var = jnp.mean(jnp.square(xf - mu), axis=-1, keepdims=True)
    return ((xf - mu) * lax.rsqrt(var + LN_EPS)).astype(x.dtype) * g + b


def rms_norm(x, g):
    xf = x.astype(F32)
    ms = jnp.mean(jnp.square(xf), axis=-1, keepdims=True)
    return (xf * lax.rsqrt(ms + LN_EPS)).astype(x.dtype) * g


def rope(x, pos):
    half = x.shape[-1] // 2
    freqs = ROPE_BASE ** (-jnp.arange(half, dtype=F32) / half)
    ang = pos.astype(F32)[:, None] * freqs[None, :]
    cos = jnp.cos(ang)[:, None, :].astype(x.dtype)
    sin = jnp.sin(ang)[:, None, :].astype(x.dtype)
    x1, x2 = x[..., :half], x[..., half:]
    return jnp.concatenate([x1 * cos - x2 * sin, x1 * sin + x2 * cos], axis=-1)


def rel_bucket(rel):
    nb = REL_BUCKETS // 2
    max_exact = nb // 2
    ret = jnp.where(rel > 0, nb, 0)
    n = jnp.abs(rel)
    nf = jnp.maximum(n, 1).astype(F32)
    large = max_exact + (jnp.log(nf / max_exact) / math.log(REL_MAX_DIST / max_exact)
                         * (nb - max_exact)).astype(jnp.int32)
    large = jnp.minimum(large, nb - 1)
    return ret + jnp.where(n < max_exact, n, large)


def chunk_visible(q_pos, k_pos):
    return (k_pos // CHUNK) <= (q_pos // CHUNK)


def over_query_blocks(fn, q_pos, *q_args):
    T = q_pos.shape[0]
    if T <= Q_BLOCK:
        return fn(q_pos, *q_args)
    nb = T // Q_BLOCK
    def split(a):
        return jnp.moveaxis(a.reshape((a.shape[0], nb, Q_BLOCK) + a.shape[2:]), 1, 0)
    xs = (q_pos.reshape(nb, Q_BLOCK),) + tuple(split(a) for a in q_args)
    out = lax.map(lambda args: fn(*args), xs)
    out = jnp.moveaxis(out, 0, 1)
    return out.reshape((out.shape[0], T) + out.shape[3:])


def mla_attend(q_pos, qb, k, v, k_pos):
    s = jnp.einsum('bqhd,bshd->bhqs', qb, k).astype(F32) * A_SCALE
    vis = chunk_visible(q_pos[:, None], k_pos[None, :])
    s = jnp.where(vis[None, None], s, -jnp.inf)
    p = jax.nn.softmax(s, axis=-1).astype(v.dtype)
    return jnp.einsum('bhqs,bshd->bqhd', p, v)


def dsa_attend(q_pos, qb, qib, wib, k, v, k_idx, k_pos, rel_bias, topk):
    B = qb.shape[0]
    vis = chunk_visible(q_pos[:, None], k_pos[None, :])
    iscore = jax.nn.relu(jnp.einsum('bqhd,bsd->bqhs', qib, k_idx).astype(F32))
    iscore = jnp.einsum('bqh,bqhs->bqs', wib.astype(F32), iscore)
    iscore = jnp.where(vis[None], iscore, -jnp.inf)
    _, sel = lax.top_k(iscore, topk)
    bidx = jnp.arange(B)[:, None, None]
    k_sel = k[bidx, sel]
    v_sel = v[bidx, sel]
    sel_pos = k_pos[sel]
    sel_vis = chunk_visible(q_pos[None, :, None], sel_pos)
    logits = jnp.einsum('bqhd,bqkhd->bqhk', qb, k_sel).astype(F32) * B_SCALE
    bias = rel_bias[rel_bucket(sel_pos - q_pos[None, :, None])].astype(F32)
    logits = logits + jnp.moveaxis(bias, -1, 2)
    logits = jnp.where(sel_vis[:, :, None, :], logits, -jnp.inf)
    p = jax.nn.softmax(logits, axis=-1).astype(v.dtype)
    return jnp.einsum('bqhk,bqkhd->bqhd', p, v_sel)


def pool_mix(u, prev, q_pos, pool_w, pool_scale):
    B, T, C = u.shape
    ext = jnp.concatenate([prev, u], axis=1)
    cs = jnp.concatenate([jnp.zeros((B, 1, C), F32), jnp.cumsum(ext.astype(F32), axis=1)], axis=1)
    outs = []
    for g, w in enumerate(POOL_WINDOWS):
        lo, hi = g * POOL_GROUP, (g + 1) * POOL_GROUP
        s = (cs[:, POOL_STATE + 1:POOL_STATE + 1 + T, lo:hi]
             - cs[:, POOL_STATE + 1 - w:POOL_STATE + 1 - w + T, lo:hi])
        cnt = jnp.minimum(w, q_pos + 1).astype(F32)[None, :, None]
        outs.append(s / cnt)
    pooled = jnp.concatenate(outs, axis=-1).astype(u.dtype) - u
    pooled = pooled.reshape(B, T, len(POOL_WINDOWS), POOL_GROUP)
    mixed = jnp.einsum('btgc,gcd->btgd', pooled, pool_w).reshape(B, T, C)
    return mixed * pool_scale, ext[:, -POOL_STATE:]


def causal_dwconv(u, prev, w, b):
    C = u.shape[-1]
    ext = jnp.concatenate([prev, u], axis=1)
    out = lax.conv_general_dilated(ext, w[:, None, :], window_strides=(1,), padding='VALID',
                                   dimension_numbers=('NWC', 'WIO', 'NWC'),
                                   feature_group_count=C)
    return out + b, ext[:, -(w.shape[0] - 1):]


def trunk_layer(x, past_ckv, past_krope, past_bk, past_bv, past_bkidx,
                past_pool, past_conv, past_ffn, lw, rel_bias, topk):
    (w_in, a_q_norm, a_kv_norm, a_w_qup, a_w_kvup, pool_w, pool_scale,
     conv_w, conv_b, conv_ln_g, conv_ln_b, w_branch, w_out, ln1_g, ln1_b,
     w_up, ffn_conv_w, ffn_conv_b, w_down, ln2_g, ln2_b) = lw
    B, T, _ = x.shape
    P = past_ckv.shape[1]
    q_pos = P + jnp.arange(T, dtype=jnp.int32)
    k_pos = jnp.arange(P + T, dtype=jnp.int32)

    proj = x @ w_in
    (cq, ckv, krope, bq, bk, bv, qidx, kidx, widx, upool, uconv, gates) = jnp.split(proj, IN_OFFSETS, axis=-1)

    cq = rms_norm(cq, a_q_norm)
    ckv = rms_norm(ckv, a_kv_norm)
    krope = rope(krope[:, :, None, :], q_pos)[:, :, 0, :]
    qa = (cq @ a_w_qup).reshape(B, T, A_HEADS, A_NOPE + A_ROPE)
    qa = jnp.concatenate([qa[..., :A_NOPE], rope(qa[..., A_NOPE:], q_pos)], axis=-1)
    ckv_all = jnp.concatenate([past_ckv, ckv], axis=1)
    krope_all = jnp.concatenate([past_krope, krope], axis=1)
    kv = (ckv_all @ a_w_kvup).reshape(B, P + T, A_HEADS, A_NOPE + A_V)
    ka = jnp.concatenate([kv[..., :A_NOPE],
                          jnp.broadcast_to(krope_all[:, :, None, :], (B, P + T, A_HEADS, A_ROPE))], axis=-1)
    va = kv[..., A_NOPE:]
    out_a = over_query_blocks(lambda qp, qb: mla_attend(qp, qb, ka, va, k_pos), q_pos, qa)

    bq = bq.reshape(B, T, B_HEADS, B_DIM)
    bk = bk.reshape(B, T, B_HEADS, B_DIM)
    bv = bv.reshape(B, T, B_HEADS, B_DIM)
    bk_all = jnp.concatenate([past_bk, bk], axis=1)
    bv_all = jnp.concatenate([past_bv, bv], axis=1)
    kidx_all = jnp.concatenate([past_bkidx, kidx], axis=1)
    qidx = qidx.reshape(B, T, IDX_HEADS, IDX_DIM)
    widx = widx * IDX_SCALE
    out_b = over_query_blocks(
        lambda qp, qb, qib, wib: dsa_attend(qp, qb, qib, wib, bk_all, bv_all, kidx_all, k_pos, rel_bias, topk),
        q_pos, bq, qidx, widx)

    out_c, new_pool = pool_mix(upool, past_pool, q_pos, pool_w, pool_scale)

    ga, gg = jnp.split(uconv, 2, axis=-1)
    glu = ga * jax.nn.sigmoid(gg)
    conv_out, new_conv = causal_dwconv(glu, past_conv, conv_w, conv_b)
    out_d = jax.nn.silu(layer_norm(conv_out, conv_ln_g, conv_ln_b))

    branches = jnp.stack([out_a.reshape(B, T, A_HEADS * A_V), out_b.reshape(B, T, B_HEADS * B_DIM),
                          out_c, out_d], axis=2)
    br = jnp.einsum('btnc,ncd->btnd', branches, w_branch)
    g = jax.nn.sigmoid(gates.reshape(B, T, N_BRANCH, D_MODEL))
    mixed = jnp.sum(g * br, axis=2) @ w_out
    x = layer_norm(ALPHA * x + mixed, ln1_g, ln1_b)

    up = x @ w_up
    upc, new_ffn = causal_dwconv(up, past_ffn, ffn_conv_w, ffn_conv_b)
    val, gate = jnp.split(upc, 2, axis=-1)
    h = (jax.nn.silu(gate) * val) @ w_down
    x = layer_norm(ALPHA * x + h, ln2_g, ln2_b)
    return x, (ckv, krope, bk, bv, kidx, new_pool, new_conv, new_ffn)


def setup_inputs(seed: int = 0) -> dict:
    key = jax.random.key(seed)
    ks = iter(jax.random.split(key, 40))
    def nrm(shape, scale):
        return jax.random.normal(next(ks), shape, F32) * scale
    def gain(shape):
        return 1.0 + nrm(shape, 0.05)
    L = DEPTH
    return {
        'x_prompt': nrm((BATCH, SEQ, D_MODEL), 1.0),
        'x_sample': nrm((DEC_BATCH, DEC_SEQ, D_MODEL), 1.0),
        'cache_a_ckv': nrm((L, DEC_BATCH, PAST_LEN, A_KV_LORA), 1.0),
        'cache_a_krope': nrm((L, DEC_BATCH, PAST_LEN, A_ROPE), 1.0),
        'cache_b_k': nrm((L, DEC_BATCH, PAST_LEN, B_HEADS, B_DIM), 1.0),
        'cache_b_v': nrm((L, DEC_BATCH, PAST_LEN, B_HEADS, B_DIM), 1.0),
        'cache_b_kidx': nrm((L, DEC_BATCH, PAST_LEN, IDX_DIM), 1.0),
        'state_pool': nrm((L, DEC_BATCH, POOL_STATE, POOL_WIDTH), 1.0),
        'state_conv': nrm((L, DEC_BATCH, CONV_K - 1, CONV_WIDTH), 1.0),
        'state_ffn': nrm((L, DEC_BATCH, FFN_K - 1, 2 * D_FF), 1.0),
        'rel_bias': nrm((REL_BUCKETS, B_HEADS), 0.5),
        'ln_in_g': gain((D_MODEL,)),
        'ln_in_b': nrm((D_MODEL,), 0.02),
        'w_in': nrm((L, D_MODEL, D_IN), D_MODEL ** -0.5),
        'a_q_norm': gain((L, A_Q_LORA)),
        'a_kv_norm': gain((L, A_KV_LORA)),
        'a_w_qup': nrm((L, A_Q_LORA, A_HEADS * (A_NOPE + A_ROPE)), A_Q_LORA ** -0.5),
        'a_w_kvup': nrm((L, A_KV_LORA, A_HEADS * (A_NOPE + A_V)), A_KV_LORA ** -0.5),
        'pool_w': nrm((L, len(POOL_WINDOWS), POOL_GROUP, POOL_GROUP), POOL_GROUP ** -0.5),
        'pool_scale': 1.0 + nrm((L, POOL_WIDTH), 0.1),
        'conv_w': nrm((L, CONV_K, CONV_WIDTH), CONV_K ** -0.5),
        'conv_b': nrm((L, CONV_WIDTH), 0.02),
        'conv_ln_g': gain((L, CONV_WIDTH)),
        'conv_ln_b': nrm((L, CONV_WIDTH), 0.02),
        'w_branch': nrm((L, N_BRANCH, 256, D_MODEL), BETA * 256 ** -0.5),
        'w_out': nrm((L, D_MODEL, D_MODEL), BETA * D_MODEL ** -0.5),
        'ln1_g': gain((L, D_MODEL)),
        'ln1_b': nrm((L, D_MODEL), 0.02),
        'w_up': nrm((L, D_MODEL, 2 * D_FF), D_MODEL ** -0.5),
        'ffn_conv_w': nrm((L, FFN_K, 2 * D_FF), FFN_K ** -0.5),
        'ffn_conv_b': nrm((L, 2 * D_FF), 0.02),
        'w_down': nrm((L, D_FF, D_MODEL), BETA * D_FF ** -0.5),
        'ln2_g': gain((L, D_MODEL)),
        'ln2_b': nrm((L, D_MODEL), 0.02),
    }


def reference(x_prompt, x_sample, cache_a_ckv, cache_a_krope, cache_b_k, cache_b_v, cache_b_kidx,
              state_pool, state_conv, state_ffn, rel_bias, ln_in_g, ln_in_b, w_in, a_q_norm, a_kv_norm,
              a_w_qup, a_w_kvup, pool_w, pool_scale, conv_w, conv_b, conv_ln_g, conv_ln_b, w_branch,
              w_out, ln1_g, ln1_b, w_up, ffn_conv_w, ffn_conv_b, w_down, ln2_g, ln2_b):
    xp = layer_norm(x_prompt, ln_in_g, ln_in_b)
    xs = layer_norm(x_sample, ln_in_g, ln_in_b)
    dt = xp.dtype
    Bp, Tp = xp.shape[0], xp.shape[1]
    Ts = xs.shape[1]
    topk_p = min(TOPK_MAX, Tp // 4)
    topk_s = min(TOPK_MAX, (PAST_LEN + Ts) // 4)
    z_ckv = jnp.zeros((Bp, 0, A_KV_LORA), dt)
    z_krope = jnp.zeros((Bp, 0, A_ROPE), dt)
    z_bk = jnp.zeros((Bp, 0, B_HEADS, B_DIM), dt)
    z_kidx = jnp.zeros((Bp, 0, IDX_DIM), dt)
    z_pool = jnp.zeros((Bp, POOL_STATE, POOL_WIDTH), dt)
    z_conv = jnp.zeros((Bp, CONV_K - 1, CONV_WIDTH), dt)
    z_ffn = jnp.zeros((Bp, FFN_K - 1, 2 * D_FF), dt)
    p_states, s_states = [], []
    for l in range(DEPTH):
        lw = (w_in[l], a_q_norm[l], a_kv_norm[l], a_w_qup[l], a_w_kvup[l], pool_w[l], pool_scale[l],
              conv_w[l], conv_b[l], conv_ln_g[l], conv_ln_b[l], w_branch[l], w_out[l], ln1_g[l], ln1_b[l],
              w_up[l], ffn_conv_w[l], ffn_conv_b[l], w_down[l], ln2_g[l], ln2_b[l])
        xp, st_p = trunk_layer(xp, z_ckv, z_krope, z_bk, z_bk, z_kidx, z_pool, z_conv, z_ffn,
                               lw, rel_bias, topk_p)
        xs, st_s = trunk_layer(xs, cache_a_ckv[l], cache_a_krope[l], cache_b_k[l], cache_b_v[l],
                               cache_b_kidx[l], state_pool[l], state_conv[l], state_ffn[l],
                               lw, rel_bias, topk_s)
        p_states.append(st_p)
        s_states.append(st_s)
    p_ckv, p_krope, p_bk, p_bv, p_kidx, p_pool, p_conv, p_ffn = [jnp.stack(a) for a in zip(*p_states)]
    s_ckv, s_krope, s_bk, s_bv, s_kidx, s_pool, s_conv, s_ffn = [jnp.stack(a) for a in zip(*s_states)]
    return (xp, xs, p_ckv, p_krope, p_bk, p_bv, p_kidx, p_pool, p_conv, p_ffn,
            s_ckv, s_krope, s_bk, s_bv, s_kidx, s_pool, s_conv, s_ffn)
```

```python
import functools

import numpy as np
import jax
import jax.numpy as jnp
from jax import lax
from jax.experimental import pallas as pl
from jax.experimental.pallas import tpu as pltpu

F32 = jnp.float32
BF16 = jnp.bfloat16

D_MODEL = 1024
DEPTH = 4
CHUNK = 64
CHUNK_SHIFT = 6
N_BRANCH = 4
A_HEADS = 4
A_NOPE = 64
A_ROPE = 32
A_V = 64
A_Q_LORA = 192
A_KV_LORA = 128
A_SCALE = (A_NOPE + A_ROPE) ** -0.5
ROPE_BASE = 10000.0
B_HEADS = 4
B_DIM = 64
B_SCALE = B_DIM ** -0.5
IDX_HEADS = 4
IDX_DIM = 32
IDX_SCALE = (IDX_HEADS ** -0.5) * (IDX_DIM ** -0.5)
TOPK_MAX = 256
REL_BUCKETS = 32
POOL_WINDOWS = (2, 4, 8, 16)
POOL_WIDTH = 256
POOL_GROUP = 64
POOL_STATE = 15
CONV_WIDTH = 256
CONV_K = 31
D_FF = 2816
FFN_K = 3
ALPHA = (2 * DEPTH) ** 0.25
LN_EPS = 1e-5

IN_SPLITS = (A_Q_LORA, A_KV_LORA, A_ROPE, 256, 256, 256, 128, IDX_DIM, IDX_HEADS,
             POOL_WIDTH, 2 * CONV_WIDTH, N_BRANCH * D_MODEL)
_OFF = [0] + [int(v) for v in np.cumsum(IN_SPLITS)]
(O_CQ, O_CKV, O_KROPE, O_BQ, O_BK, O_BV, O_QIDX, O_KIDX, O_WIDX, O_POOL, O_CONV, O_GATES,
 D_IN) = _OFF

PROJ_W = 6400
CB_GATES = 0
CB_CQ = 16
CB_CK = 17
CB_BQ = 18
CB_BK = 19
CB_BV = 20
CB_IDX = 21
CB_CONV = 11
CB_POOL = 24

VMEM_LIMIT_BYTES = 48 * 1024 * 1024
LANES = 128

NEG = -1e30
M_INIT = -5e29
SEARCH_MAX_IT = 320
SNAP_EVERY = 4

_NT = (((1,), (1,)), ((), ()))


def _cparams(sem):
    return pltpu.CompilerParams(dimension_semantics=sem, vmem_limit_bytes=VMEM_LIMIT_BYTES)


def _ln_rows(x, g, b):
    mu = jnp.mean(x, axis=-1, keepdims=True)
    xc = x - mu
    var = jnp.mean(xc * xc, axis=-1, keepdims=True)
    return xc * lax.rsqrt(var + LN_EPS) * g + b


def _sigmoid(x):
    return 1.0 / (1.0 + jnp.exp(-x))


def _ln_kernel(x_ref, g_ref, b_ref, o_ref):
    o_ref[...] = _ln_rows(x_ref[...], g_ref[...], b_ref[...])


def _layer_norm(x2d, g, b, tm):
    n, d = x2d.shape
    return pl.pallas_call(
        _ln_kernel,
        out_shape=jax.ShapeDtypeStruct((n, d), F32),
        grid=(n // tm,),
        in_specs=[pl.BlockSpec((tm, d), lambda i: (i, 0)),
                  pl.BlockSpec((1, d), lambda i: (0, 0)),
                  pl.BlockSpec((1, d), lambda i: (0, 0))],
        out_specs=pl.BlockSpec((tm, d), lambda i: (i, 0)),
        compiler_params=_cparams(("parallel",)),
        name="input_ln",
    )(x2d, g.reshape(1, d), b.reshape(1, d))


def _inproj_kernel(x_ref, w_ref, o_ref, xb_ref):
    @pl.when(pl.program_id(1) == 0)
    def _():
        xb_ref[...] = x_ref[...].astype(BF16)

    o_ref[...] = jnp.dot(xb_ref[...], w_ref[...], preferred_element_type=F32)


def _inproj(x2d, w, tm, tn):
    n, k = x2d.shape
    nw = w.shape[1]
    return pl.pallas_call(
        _inproj_kernel,
        out_shape=jax.ShapeDtypeStruct((n, nw), F32),
        grid=(n // tm, nw // tn),
        in_specs=[pl.BlockSpec((tm, k), lambda i, j: (i, 0)),
                  pl.BlockSpec((k, tn), lambda i, j: (0, j))],
        out_specs=pl.BlockSpec((tm, tn), lambda i, j: (i, j)),
        scratch_shapes=[pltpu.VMEM((tm, k), BF16)],
        compiler_params=_cparams(("parallel", "arbitrary")),
        name="in_proj",
    )(x2d, w)


def _aprep_kernel(cq_ref, ck_ref, cosq_ref, sinq_ref, cosk_ref, sink_ref, gq_ref, gkv_ref,
                  wq_ref, wqr_ref, qa_ref, ckv_ref, kr_ref):
    cq = cq_ref[...]
    ms = jnp.sum(cq * cq, axis=-1, keepdims=True) * (1.0 / A_Q_LORA)
    cqb = ((cq * lax.rsqrt(ms + LN_EPS)) * gq_ref[...]).astype(BF16)
    q = jnp.dot(cqb, wq_ref[...], preferred_element_type=F32)
    qr = jnp.dot(cqb, wqr_ref[...], preferred_element_type=F32)
    cos = jnp.concatenate([cosq_ref[...]] * A_HEADS, axis=1)
    sin = jnp.concatenate([sinq_ref[...]] * A_HEADS, axis=1)
    qa_ref[...] = ((q * cos + qr * sin) * A_SCALE).astype(BF16)

    ck = ck_ref[...]
    ckv = ck[:, :A_KV_LORA]
    ms2 = jnp.mean(ckv * ckv, axis=-1, keepdims=True)
    ckv_ref[...] = (ckv * lax.rsqrt(ms2 + LN_EPS)) * gkv_ref[...]
    kr = ck[:, A_KV_LORA:]
    krr = pltpu.roll(kr, LANES - A_ROPE, 1)
    kr_ref[...] = kr * cosk_ref[...] + krr * sink_ref[...]


def _aprep(proj3, tabs, gq, gkv, wq, wqr, tm):
    b, t, _ = proj3.shape
    cosq, sinq, cosk, sink = tabs
    tab_spec = pl.BlockSpec((tm, LANES), lambda bi, i: (i, 0))
    full = lambda shape: pl.BlockSpec(shape, lambda bi, i: (0,) * len(shape))
    return pl.pallas_call(
        _aprep_kernel,
        out_shape=(jax.ShapeDtypeStruct((b, t, 512), BF16),
                   jax.ShapeDtypeStruct((b, t, A_KV_LORA), F32),
                   jax.ShapeDtypeStruct((b, t, LANES), F32)),
        grid=(b, t // tm),
        in_specs=[pl.BlockSpec((None, tm, 256), lambda bi, i: (bi, i, CB_CQ)),
                  pl.BlockSpec((None, tm, 256), lambda bi, i: (bi, i, CB_CK)),
                  tab_spec, tab_spec, tab_spec, tab_spec,
                  full((1, 256)), full((1, A_KV_LORA)), full((256, 512)), full((256, 512))],
        out_specs=(pl.BlockSpec((None, tm, 512), lambda bi, i: (bi, i, 0)),
                   pl.BlockSpec((None, tm, A_KV_LORA), lambda bi, i: (bi, i, 0)),
                   pl.BlockSpec((None, tm, LANES), lambda bi, i: (bi, i, 0))),
        compiler_params=_cparams(("parallel", "parallel")),
        name="mla_prep",
    )(proj3, proj3, cosq, sinq, cosk, sink, gq, gkv, wq, wqr)


def _kvup_kernel(ckv_ref, kr_ref, wkn_ref, wv_ref, e_ref, k_ref, v_ref):
    cb = ckv_ref[...].astype(BF16)
    k = jnp.dot(cb, wkn_ref[...], preferred_element_type=F32)
    k = k + jnp.dot(kr_ref[...].astype(BF16), e_ref[...], preferred_element_type=F32)
    k_ref[...] = k.astype(BF16)
    v_ref[...] = jnp.dot(cb, wv_ref[...], preferred_element_type=F32).astype(BF16)


def _kvup(ckv2d, kr2d, wkn, wv, e, tm):
    m = ckv2d.shape[0]
    full = lambda shape: pl.BlockSpec(shape, lambda i: (0,) * len(shape))
    return pl.pallas_call(
        _kvup_kernel,
        out_shape=(jax.ShapeDtypeStruct((m, 512), BF16), jax.ShapeDtypeStruct((m, 256), BF16)),
        grid=(m // tm,),
        in_specs=[pl.BlockSpec((tm, A_KV_LORA), lambda i: (i, 0)),
                  pl.BlockSpec((tm, LANES), lambda i: (i, 0)),
                  full((A_KV_LORA, 512)), full((A_KV_LORA, 256)), full((LANES, 512))],
        out_specs=(pl.BlockSpec((tm, 512), lambda i: (i, 0)),
                   pl.BlockSpec((tm, 256), lambda i: (i, 0))),
        compiler_params=_cparams(("parallel",)),
        name="kv_up",
    )(ckv2d, kr2d, wkn, wv, e)


def _online(carry, s, v):
    m, l, acc = carry
    m_new = jnp.maximum(m, jnp.max(s, axis=1, keepdims=True))
    alpha = jnp.exp(m - m_new)
    p = jnp.exp(s - m_new)
    l = alpha * l + jnp.sum(p, axis=1, keepdims=True)
    acc = alpha * acc + jnp.dot(p.astype(BF16), v, preferred_element_type=F32)
    return m_new, l, acc


def _softmax_init(tq, dv):
    return (jnp.full((tq, 1), M_INIT, F32), jnp.zeros((tq, 1), F32), jnp.zeros((tq, dv), F32))


def _visible_limit(qpos, s_valid):
    return jnp.minimum((jnp.right_shift(qpos, CHUNK_SHIFT) + 1) * CHUNK, s_valid)


def _mla_kernel(q_ref, k_ref, v_ref, o_ref, *, tq, tk, past, s_valid):
    i = pl.program_id(1)
    q0 = past + i * tq
    qpos = q0 + lax.broadcasted_iota(jnp.int32, (tq, 1), 0)
    q_lim = _visible_limit(qpos, s_valid)
    lim_first = _visible_limit(q0, s_valid)
    lim_last = _visible_limit(q0 + tq - 1, s_valid)
    n_full = lim_first // tk
    n_tot = (lim_last + tk - 1) // tk
    q = q_ref[...]

    for h in range(A_HEADS):
        qh = q[:, h * LANES:(h + 1) * LANES]

        def body(j, carry, masked, h=h, qh=qh):
            ks = pl.multiple_of(j * tk, tk)
            k = k_ref[pl.ds(ks, tk), h * LANES:(h + 1) * LANES]
            v = v_ref[pl.ds(ks, tk), h * A_V:(h + 1) * A_V]
            s = lax.dot_general(qh, k, _NT, preferred_element_type=F32)
            if masked:
                kpos = ks + lax.broadcasted_iota(jnp.int32, (1, tk), 1)
                s = jnp.where(kpos < q_lim, s, NEG)
            return _online(carry, s, v)

        carry = lax.fori_loop(0, n_full, functools.partial(body, masked=False),
                              _softmax_init(tq, A_V))
        _, l, acc = lax.fori_loop(n_full, n_tot, functools.partial(body, masked=True), carry)
        o_ref[:, h * A_V:(h + 1) * A_V] = acc / l


def _mla(qa, kx, v, tq, tk, past, s_valid):
    b, t, _ = qa.shape
    s_pad = kx.shape[1]
    kern = functools.partial(_mla_kernel, tq=tq, tk=tk, past=past, s_valid=s_valid)
    return pl.pallas_call(
        kern,
        out_shape=jax.ShapeDtypeStruct((b, t, 256), F32),
        grid=(b, t // tq),
        in_specs=[pl.BlockSpec((None, tq, 512), lambda bi, i: (bi, i, 0)),
                  pl.BlockSpec((None, s_pad, 512), lambda bi, i: (bi, 0, 0)),
                  pl.BlockSpec((None, s_pad, 256), lambda bi, i: (bi, 0, 0))],
        out_specs=pl.BlockSpec((None, tq, 256), lambda bi, i: (bi, i, 0)),
        compiler_params=_cparams(("parallel", "arbitrary")),
        name="mla_attn",
    )(qa, kx, v)


def _dsa_kernel(bq_ref, idx_ref, bk_ref, bv_ref, ki_ref, bias_ref, o_ref, sc_ref, *,
                tq, tkc, tkf, past, s_valid, topk):
    i = pl.program_id(1)
    q0 = past + i * tq
    qpos = q0 + lax.broadcasted_iota(jnp.int32, (tq, 1), 0)
    q_lim = _visible_limit(qpos, s_valid)
    lim_last = _visible_limit(q0 + tq - 1, s_valid)
    nch = (lim_last + tkc - 1) // tkc
    ngrp = tkc // LANES
    kf = float(topk)

    def chunk(c):
        return pl.ds(pl.multiple_of(c * tkc, tkc), tkc)

    def key_pos(c):
        return c * tkc + lax.broadcasted_iota(jnp.int32, (1, tkc), 1)

    idx = idx_ref[...]
    lane = lax.broadcasted_iota(jnp.int32, (1, LANES), 1)
    qi = idx[:, :LANES]
    qhs = [jnp.where((lane >= h * IDX_DIM) & (lane < (h + 1) * IDX_DIM), qi, 0.0).astype(BF16)
           for h in range(IDX_HEADS)]
    wi = idx[:, LANES + IDX_DIM:LANES + IDX_DIM + IDX_HEADS] * IDX_SCALE
    whs = [wi[:, h:h + 1] for h in range(IDX_HEADS)]

    def score_chunk(c, carry):
        ki = ki_ref[chunk(c), :]
        tot = None
        for h in range(IDX_HEADS):
            d = lax.dot_general(qhs[h], ki, _NT, preferred_element_type=F32)
            term = whs[h] * jnp.maximum(d, 0.0)
            tot = term if tot is None else tot + term
        sc_ref[:, chunk(c)] = jnp.where(key_pos(c) < q_lim, tot, -jnp.inf)
        return carry

    lax.fori_loop(0, nch, score_chunk, 0)

    def lane_fold(x, op):
        r = x[:, :LANES]
        for g in range(1, ngrp):
            r = op(r, x[:, g * LANES:(g + 1) * LANES])
        return r

    def count_where(pred):
        def body(c, acc):
            m = jnp.where(pred(sc_ref[:, chunk(c)], c), 1.0, 0.0)
            return acc + lane_fold(m, jnp.add)
        acc = lax.fori_loop(0, nch, body, jnp.zeros((tq, LANES), F32))
        return jnp.sum(acc, axis=1, keepdims=True)

    def count_gt(t):
        return count_where(lambda x, c: x > t)

    def search():
        def mm_body(c, carry):
            mn, mx = carry
            x = sc_ref[:, chunk(c)]
            xm = jnp.where(x == -jnp.inf, jnp.inf, x)
            return (jnp.minimum(mn, lane_fold(xm, jnp.minimum)),
                    jnp.maximum(mx, lane_fold(x, jnp.maximum)))
        mn, mx = lax.fori_loop(0, nch, mm_body, (jnp.full((tq, LANES), jnp.inf, F32),
                                                 jnp.full((tq, LANES), -jnp.inf, F32)))
        mn = jnp.min(mn, axis=1, keepdims=True)
        mx = jnp.max(mx, axis=1, keepdims=True)
        lo0 = mn - (1.0 + jnp.abs(mn) * (2.0 ** -10))
        done0 = jnp.where(q_lim.astype(F32) <= kf, 1.0, 0.0)
        zeros = jnp.zeros((tq, 1), F32)

        def snap(lo, hi, t, done, tie, vt):
            def body(c, carry):
                a, b = carry
                x = sc_ref[:, chunk(c)]
                xa = jnp.where(x > lo, x, jnp.inf)
                xb = jnp.where(x <= hi, x, -jnp.inf)
                return (jnp.minimum(a, lane_fold(xa, jnp.minimum)),
                        jnp.maximum(b, lane_fold(xb, jnp.maximum)))
            a, b = lax.fori_loop(0, nch, body, (jnp.full((tq, LANES), jnp.inf, F32),
                                                jnp.full((tq, LANES), -jnp.inf, F32)))
            a = jnp.min(a, axis=1, keepdims=True)
            b = jnp.max(b, axis=1, keepdims=True)
            nd = done < 0.5
            new_tie = nd & (a >= b)
            hi = jnp.where(nd & (a < b), b, hi)
            vt = jnp.where(new_tie, b, vt)
            tie = jnp.where(new_tie, 1.0, tie)
            done = jnp.where(new_tie, 1.0, done)
            return lo, hi, t, done, tie, vt

        def cond(s):
            return (s[1] > 0) & (s[0] < SEARCH_MAX_IT)

        def body(s):
            it, _, lo, hi, t, done, tie, vt = s
            mid = 0.5 * lo + 0.5 * hi
            stuck = (mid <= lo) | (mid >= hi)
            c = count_gt(mid)
            nd = done < 0.5
            live = nd & jnp.logical_not(stuck)
            hit = live & (c == kf)
            stk = nd & stuck
            t = jnp.where(hit, mid, t)
            vt = jnp.where(stk, hi, vt)
            tie = jnp.where(stk, 1.0, tie)
            lo = jnp.where(live & (c > kf), mid, lo)
            hi = jnp.where(live & (c < kf), mid, hi)
            done = jnp.where(hit | stk, 1.0, done)
            lo, hi, t, done, tie, vt = lax.cond(
                it % SNAP_EVERY == SNAP_EVERY - 1, snap,
                lambda *a: a, lo, hi, t, done, tie, vt)
            active = (jnp.max(1.0 - done) > 0.5).astype(jnp.int32)
            return it + 1, active, lo, hi, t, done, tie, vt

        active0 = (jnp.max(1.0 - done0) > 0.5).astype(jnp.int32)
        s = lax.while_loop(cond, body, (jnp.int32(0), active0, lo0, mx, lo0, done0, zeros, zeros))
        _, _, _, _, t, _, tie, vt = s

        def tie_cut():
            t2 = jnp.where(tie > 0.5, vt, t)
            need = kf - count_gt(t2)

            def bs_body(_, carry):
                lo_c, hi_c = carry
                mid_c = jnp.right_shift(lo_c + hi_c, 1)
                cnt = count_where(lambda x, c: (x == vt) & (key_pos(c) < mid_c))
                ok = cnt >= need
                return jnp.where(ok, lo_c, mid_c), jnp.where(ok, mid_c, hi_c)

            nbits = int(s_valid).bit_length()
            _, cut = lax.fori_loop(0, nbits, bs_body,
                                   (jnp.zeros((tq, 1), jnp.int32),
                                    jnp.full((tq, 1), 1 << nbits, jnp.int32)))
            return t2, jnp.where(tie > 0.5, cut, 0)

        any_tie = jnp.max(tie) > 0.5
        t, cut = lax.cond(any_tie, tie_cut, lambda: (t, jnp.zeros((tq, 1), jnp.int32)))
        return t, tie, vt, cut

    def no_search():
        z = jnp.zeros((tq, 1), F32)
        return jnp.full((tq, 1), -jnp.inf, F32), z, z, jnp.zeros((tq, 1), jnp.int32)

    t, tie, vt, cut = lax.cond(lim_last > topk, search, no_search)
    tied = tie > 0.5

    def to_mask(c, carry):
        x = sc_ref[:, chunk(c)]
        sel = (x > t) | (tied & (x == vt) & (key_pos(c) < cut))
        sc_ref[:, chunk(c)] = jnp.where(sel, 0.0, NEG)
        return carry

    lax.fori_loop(0, nch, to_mask, 0)

    bq = (bq_ref[...] * B_SCALE).astype(BF16)
    nfar = jnp.maximum(q0 - (LANES - 1), 0) // tkf
    j_start = nfar * (tkf // LANES)
    j_end = (lim_last + LANES - 1) // LANES
    d0 = q0 // LANES

    for h in range(B_HEADS):
        qh = bq[:, h * B_DIM:(h + 1) * B_DIM]
        cbias = bias_ref[2, h, 0:1, 0:1]

        def far_body(j, carry, h=h, qh=qh, cbias=cbias):
            ks = pl.ds(pl.multiple_of(j * tkf, tkf), tkf)
            k = bk_ref[ks, h * B_DIM:(h + 1) * B_DIM]
            v = bv_ref[ks, h * B_DIM:(h + 1) * B_DIM]
            s = lax.dot_general(qh, k, _NT, preferred_element_type=F32)
            return _online(carry, s + (sc_ref[:, ks] + cbias), v)

        def near_body(j, carry, h=h, qh=qh):
            ks = pl.ds(pl.multiple_of(j * LANES, LANES), LANES)
            k = bk_ref[ks, h * B_DIM:(h + 1) * B_DIM]
            v = bv_ref[ks, h * B_DIM:(h + 1) * B_DIM]
            s = lax.dot_general(qh, k, _NT, preferred_element_type=F32)
            bias = bias_ref[jnp.minimum(d0 - j, 2), h]
            return _online(carry, s + (sc_ref[:, ks] + bias), v)

        carry = lax.fori_loop(0, nfar, far_body, _softmax_init(tq, B_DIM))
        _, l, acc = lax.fori_loop(j_start, j_end, near_body, carry)
        o_ref[:, h * B_DIM:(h + 1) * B_DIM] = acc / l


def _dsa(proj3, bk, bv, ki, bias, tq, tkc, tkf, past, s_valid, topk):
    b, t, _ = proj3.shape
    s_pad = bk.shape[1]
    assert past % LANES == 0 and (tq == LANES or t == tq) and s_pad % tkc == 0
    kern = functools.partial(_dsa_kernel, tq=tq, tkc=tkc, tkf=tkf, past=past, s_valid=s_valid,
                             topk=topk)
    return pl.pallas_call(
        kern,
        out_shape=jax.ShapeDtypeStruct((b, t, 256), F32),
        grid=(b, t // tq),
        in_specs=[pl.BlockSpec((None, tq, 256), lambda bi, i: (bi, i, CB_BQ)),
                  pl.BlockSpec((None, tq, 256), lambda bi, i: (bi, i, CB_IDX)),
                  pl.BlockSpec((None, s_pad, 256), lambda bi, i: (bi, 0, 0)),
                  pl.BlockSpec((None, s_pad, 256), lambda bi, i: (bi, 0, 0)),
                  pl.BlockSpec((None, s_pad, LANES), lambda bi, i: (bi, 0, 0)),
                  pl.BlockSpec((3, B_HEADS, tq, LANES), lambda bi, i: (0, 0, 0, 0))],
        out_specs=pl.BlockSpec((None, tq, 256), lambda bi, i: (bi, i, 0)),
        scratch_shapes=[pltpu.VMEM((tq, s_pad), F32)],
        compiler_params=_cparams(("parallel", "arbitrary")),
        name="dsa_attn",
    )(proj3, proj3, bk, bv, ki, bias)


def _pool_kernel(u_ref, prev_ref, w_ref, scale_ref, o_ref, st_ref, ext_ref, *, tr, past):
    i = pl.program_id(1)

    @pl.when(i == 0)
    def _():
        ext_ref[1:16, :] = prev_ref[...]

    u = u_ref[...]
    ext_ref[16:16 + tr, :] = u

    def tap(k, lo):
        return ext_ref[16 - k:16 - k + tr, lo:lo + LANES]

    pos1 = past + i * tr + 1 + lax.broadcasted_iota(jnp.int32, (tr, 1), 0)
    cnt = [jnp.minimum(w, pos1).astype(F32) for w in POOL_WINDOWS]
    lane = lax.broadcasted_iota(jnp.int32, (1, LANES), 1)
    first = lane < POOL_GROUP

    s2 = tap(0, 0) + tap(1, 0)
    s4 = s2 + tap(2, 0) + tap(3, 0)
    s8 = tap(0, LANES)
    for k in range(1, 8):
        s8 = s8 + tap(k, LANES)
    s16 = s8
    for k in range(8, 16):
        s16 = s16 + tap(k, LANES)
    m_lo = jnp.where(first, s2 / cnt[0], s4 / cnt[1])
    m_hi = jnp.where(first, s8 / cnt[2], s16 / cnt[3])
    pooled = jnp.concatenate([m_lo, m_hi], axis=1) - u
    mixed = jnp.dot(pooled.astype(BF16), w_ref[...], preferred_element_type=F32)
    o_ref[...] = mixed * scale_ref[...]

    new_state = ext_ref[1 + tr:16 + tr, :]
    st_ref[...] = new_state
    ext_ref[1:16, :] = new_state


def _pool(proj3, prev, w_bd, scale, tr, past):
    b, t, _ = proj3.shape
    kern = functools.partial(_pool_kernel, tr=tr, past=past)
    return pl.pallas_call(
        kern,
        out_shape=(jax.ShapeDtypeStruct((b, t, POOL_WIDTH), F32),
                   jax.ShapeDtypeStruct((b, POOL_STATE, POOL_WIDTH), F32)),
        grid=(b, t // tr),
        in_specs=[pl.BlockSpec((None, tr, 256), lambda bi, i: (bi, i, CB_POOL)),
                  pl.BlockSpec((None, POOL_STATE, POOL_WIDTH), lambda bi, i: (bi, 0, 0)),
                  pl.BlockSpec((256, 256), lambda bi, i: (0, 0)),
                  pl.BlockSpec((1, 256), lambda bi, i: (0, 0))],
        out_specs=(pl.BlockSpec((None, tr, 256), lambda bi, i: (bi, i, 0)),
                   pl.BlockSpec((None, POOL_STATE, POOL_WIDTH), lambda bi, i: (bi, 0, 0))),
        scratch_shapes=[pltpu.VMEM((tr + 16, POOL_WIDTH), F32)],
        compiler_params=_cparams(("parallel", "arbitrary")),
        name="pool_mix",
    )(proj3, prev, w_bd, scale)


def _conv_kernel(u_ref, prev_ref, w_ref, b_ref, g_ref, beta_ref, o_ref, st_ref, ext_ref, *, tr, sub):
    i = pl.program_id(1)
    npre = CONV_K - 1

    @pl.when(i == 0)
    def _():
        ext_ref[2:2 + npre, :] = prev_ref[...]

    u = u_ref[...]
    ext_ref[32:32 + tr, :] = u[:, :CONV_WIDTH] * _sigmoid(u[:, CONV_WIDTH:])

    for r0 in range(0, tr, sub):
        acc = ext_ref[r0 + 2:r0 + 2 + sub, :] * w_ref[0:1, :]
        for k in range(1, CONV_K):
            acc = acc + ext_ref[r0 + 2 + k:r0 + 2 + k + sub, :] * w_ref[k:k + 1, :]
        y = _ln_rows(acc + b_ref[...], g_ref[...], beta_ref[...])
        o_ref[r0:r0 + sub, :] = y * _sigmoid(y)

    new_state = ext_ref[2 + tr:2 + tr + npre, :]
    st_ref[...] = new_state
    ext_ref[2:2 + npre, :] = new_state


def _conv(proj3, prev, w, bias, g, beta, tr, sub):
    b, t, _ = proj3.shape
    npre = CONV_K - 1
    kern = functools.partial(_conv_kernel, tr=tr, sub=sub)
    vec = pl.BlockSpec((1, CONV_WIDTH), lambda bi, i: (0, 0))
    return pl.pallas_call(
        kern,
        out_shape=(jax.ShapeDtypeStruct((b, t, CONV_WIDTH), F32),
                   jax.ShapeDtypeStruct((b, npre, CONV_WIDTH), F32)),
        grid=(b, t // tr),
        in_specs=[pl.BlockSpec((None, tr, 512), lambda bi, i: (bi, i, CB_CONV)),
                  pl.BlockSpec((None, npre, CONV_WIDTH), lambda bi, i: (bi, 0, 0)),
                  pl.BlockSpec((CONV_K, CONV_WIDTH), lambda bi, i: (0, 0)),
                  vec, vec, vec],
        out_specs=(pl.BlockSpec((None, tr, CONV_WIDTH), lambda bi, i: (bi, i, 0)),
                   pl.BlockSpec((None, npre, CONV_WIDTH), lambda bi, i: (bi, 0, 0))),
        scratch_shapes=[pltpu.VMEM((tr + 32, CONV_WIDTH), F32)],
        compiler_params=_cparams(("parallel", "arbitrary")),
        name="conv_module",
    )(proj3, prev, w, bias, g, beta)


def _merge_kernel(a_ref, b_ref, c_ref, d_ref, gates_ref, x_ref, wb_ref, wo_ref, g_ref, beta_ref,
                  o_ref):
    mixed = None
    for n, br_ref in enumerate((a_ref, b_ref, c_ref, d_ref)):
        br = jnp.dot(br_ref[...].astype(BF16), wb_ref[n], preferred_element_type=F32)
        term = _sigmoid(gates_ref[:, n * D_MODEL:(n + 1) * D_MODEL]) * br
        mixed = term if mixed is None else mixed + term
    y = jnp.dot(mixed.astype(BF16), wo_ref[...], preferred_element_type=F32)
    o_ref[...] = _ln_rows(ALPHA * x_ref[...] + y, g_ref[...], beta_ref[...])


def _merge(oa, ob, oc, od, proj3, x3, wb, wo, g, beta, tm):
    b, t, _ = x3.shape
    br_spec = pl.BlockSpec((None, tm, 256), lambda bi, i: (bi, i, 0))
    vec = pl.BlockSpec((1, D_MODEL), lambda bi, i: (0, 0))
    return pl.pallas_call(
        _merge_kernel,
        out_shape=jax.ShapeDtypeStruct((b, t, D_MODEL), F32),
        grid=(b, t // tm),
        in_specs=[br_spec, br_spec, br_spec, br_spec,
                  pl.BlockSpec((None, tm, N_BRANCH * D_MODEL), lambda bi, i: (bi, i, CB_GATES)),
                  pl.BlockSpec((None, tm, D_MODEL), lambda bi, i: (bi, i, 0)),
                  pl.BlockSpec((N_BRANCH, 256, D_MODEL), lambda bi, i: (0, 0, 0)),
                  pl.BlockSpec((D_MODEL, D_MODEL), lambda bi, i: (0, 0)),
                  vec, vec],
        out_specs=pl.BlockSpec((None, tm, D_MODEL), lambda bi, i: (bi, i, 0)),
        compiler_params=_cparams(("parallel", "parallel")),
        name="merge_ln1",
    )(oa, ob, oc, od, proj3, x3, wb, wo, g, beta)


FF_TC = 256
FF_NC = D_FF // FF_TC


def _ffn_kernel(x_ref, stv_ref, stg_ref, wv_ref, wg_ref, cwv_ref, cwg_ref, cbv_ref, cbg_ref,
                wd_ref, g_ref, beta_ref, o_ref, nstv_ref, nstg_ref,
                xb_ref, acc_ref, carv_ref, carg_ref, ubuf_ref, *, tm):
    i = pl.program_id(1)
    c = pl.program_id(2)

    @pl.when(c == 0)
    def _():
        xb_ref[...] = x_ref[...].astype(BF16)
        acc_ref[...] = jnp.zeros_like(acc_ref)

    @pl.when(i == 0)
    def _():
        carv_ref[c, 6:8, :] = stv_ref[...]
        carg_ref[c, 6:8, :] = stg_ref[...]

    xb = xb_ref[...]

    def conv3(w_ref, cw_ref, cb_ref, car_ref, nst_ref):
        u = jnp.dot(xb, w_ref[...], preferred_element_type=F32)
        ubuf_ref[6:8, :] = car_ref[c, 6:8, :]
        ubuf_ref[8:8 + tm, :] = u
        y = (cw_ref[0:1, :] * ubuf_ref[6:6 + tm, :] + cw_ref[1:2, :] * ubuf_ref[7:7 + tm, :]
             + cw_ref[2:3, :] * u + cb_ref[...])
        last = ubuf_ref[6 + tm:8 + tm, :]
        car_ref[c, 6:8, :] = last
        nst_ref[:, pl.ds(pl.multiple_of(c * FF_TC, FF_TC), FF_TC)] = last
        return y

    val = conv3(wv_ref, cwv_ref, cbv_ref, carv_ref, nstv_ref)
    gate = conv3(wg_ref, cwg_ref, cbg_ref, carg_ref, nstg_ref)
    h = (gate * _sigmoid(gate)) * val
    acc_ref[...] += jnp.dot(h.astype(BF16), wd_ref[...], preferred_element_type=F32)

    @pl.when(c == FF_NC - 1)
    def _():
        o_ref[...] = _ln_rows(ALPHA * x_ref[...] + acc_ref[...], g_ref[...], beta_ref[...])


def _ffn(x3, st, w_up, cw, cb, w_dn, g, beta, tm):
    b, t, _ = x3.shape
    kern = functools.partial(_ffn_kernel, tm=tm)
    nk = FFN_K - 1
    vec = pl.BlockSpec((1, D_MODEL), lambda bi, i, c: (0, 0))
    st_v = pl.BlockSpec((None, nk, FF_TC), lambda bi, i, c: (bi, 0, c))
    st_g = pl.BlockSpec((None, nk, FF_TC), lambda bi, i, c: (bi, 0, FF_NC + c))
    return pl.pallas_call(
        kern,
        out_shape=(jax.ShapeDtypeStruct((b, t, D_MODEL), F32),
                   jax.ShapeDtypeStruct((b, nk, D_FF), F32),
                   jax.ShapeDtypeStruct((b, nk, D_FF), F32)),
        grid=(b, t // tm, FF_NC),
        in_specs=[pl.BlockSpec((None, tm, D_MODEL), lambda bi, i, c: (bi, i, 0)),
                  st_v, st_g,
                  pl.BlockSpec((D_MODEL, FF_TC), lambda bi, i, c: (0, c)),
                  pl.BlockSpec((D_MODEL, FF_TC), lambda bi, i, c: (0, FF_NC + c)),
                  pl.BlockSpec((FFN_K, FF_TC), lambda bi, i, c: (0, c)),
                  pl.BlockSpec((FFN_K, FF_TC), lambda bi, i, c: (0, FF_NC + c)),
                  pl.BlockSpec((1, FF_TC), lambda bi, i, c: (0, c)),
                  pl.BlockSpec((1, FF_TC), lambda bi, i, c: (0, FF_NC + c)),
                  pl.BlockSpec((FF_TC, D_MODEL), lambda bi, i, c: (c, 0)),
                  vec, vec],
        out_specs=(pl.BlockSpec((None, tm, D_MODEL), lambda bi, i, c: (bi, i, 0)),
                   pl.BlockSpec((None, nk, D_FF), lambda bi, i, c: (bi, 0, 0)),
                   pl.BlockSpec((None, nk, D_FF), lambda bi, i, c: (bi, 0, 0))),
        scratch_shapes=[pltpu.VMEM((tm, D_MODEL), BF16),
                        pltpu.VMEM((tm, D_MODEL), F32),
                        pltpu.VMEM((FF_NC, 8, FF_TC), F32),
                        pltpu.VMEM((FF_NC, 8, FF_TC), F32),
                        pltpu.VMEM((tm + 8, FF_TC), F32)],
        compiler_params=_cparams(("parallel", "arbitrary", "arbitrary")),
        name="conv_ffn_ln2",
    )(x3, st, st, w_up, w_up, cw, cw, cb, cb, w_dn, g, beta)


def _rope_tables(past, t):
    half = A_ROPE // 2
    freqs = ROPE_BASE ** (-jnp.arange(half, dtype=F32) / half)
    pos = past + jnp.arange(t, dtype=jnp.int32)
    ang = pos.astype(F32)[:, None] * freqs[None, :]
    cos, sin = jnp.cos(ang), jnp.sin(ang)
    c2 = jnp.concatenate([cos, cos], axis=1)
    s2 = jnp.concatenate([sin, sin], axis=1)
    one = jnp.ones((t, A_NOPE), F32)
    z = lambda n: jnp.zeros((t, n), F32)
    cosq = jnp.concatenate([one, c2, z(LANES - A_NOPE - A_ROPE)], axis=1)
    sinq = jnp.concatenate([z(A_NOPE), s2, z(LANES - A_NOPE - A_ROPE)], axis=1)
    cosk = jnp.concatenate([c2, z(LANES - A_ROPE)], axis=1)
    sink = jnp.concatenate([s2, z(LANES - A_ROPE)], axis=1)
    return cosq, sinq, cosk, sink


_BUCKET_EDGES = (12, 16, 23, 32, 46, 64, 91)


def _rel_bucket(rel):
    nb = REL_BUCKETS // 2
    max_exact = nb // 2
    n = np.abs(rel)
    large = max_exact + sum((n >= e).astype(np.int64) for e in _BUCKET_EDGES)
    return np.where(rel > 0, nb, 0) + np.where(n < max_exact, n, large)


def _bias_tiles(rel_bias, tq):
    r = np.arange(tq)[:, None]
    c = np.arange(LANES)[None, :]
    buckets = np.stack([_rel_bucket(c - r - LANES * d) for d in range(3)])
    tiles = rel_bias[jnp.asarray(buckets, dtype=jnp.int32)]
    return jnp.transpose(tiles, (0, 3, 1, 2))


def _pack_w_in(w):
    z = lambda n: jnp.zeros((D_MODEL, n), F32)
    kr = w[:, O_KROPE:O_KROPE + A_ROPE]
    half = A_ROPE // 2
    kr_rot = jnp.concatenate([-kr[:, half:], kr[:, :half]], axis=1)
    parts = [
        w[:, O_GATES:O_GATES + N_BRANCH * D_MODEL],
        w[:, O_CQ:O_CQ + A_Q_LORA], z(256 - A_Q_LORA),
        w[:, O_CKV:O_CKV + A_KV_LORA], kr, kr_rot, z(256 - A_KV_LORA - 2 * A_ROPE),
        w[:, O_BQ:O_BQ + 256], w[:, O_BK:O_BK + 256], w[:, O_BV:O_BV + 256],
        w[:, O_QIDX:O_QIDX + 128], w[:, O_KIDX:O_KIDX + IDX_DIM], w[:, O_WIDX:O_WIDX + IDX_HEADS],
        z(256 - 128 - IDX_DIM - IDX_HEADS),
        w[:, O_CONV:O_CONV + 2 * CONV_WIDTH],
        w[:, O_POOL:O_POOL + POOL_WIDTH],
    ]
    return jnp.concatenate(parts, axis=1).astype(BF16)


def _pack_w_qup(w):
    zrow = lambda m: jnp.zeros((256 - A_Q_LORA, m.shape[1]), F32)
    half = A_ROPE // 2
    dq = A_NOPE + A_ROPE
    cols, rcols = [], []
    for h in range(A_HEADS):
        nope = w[:, h * dq:h * dq + A_NOPE]
        rp = w[:, h * dq + A_NOPE:(h + 1) * dq]
        pad = jnp.zeros((A_Q_LORA, LANES - dq), F32)
        cols += [nope, rp, pad]
        rcols += [jnp.zeros_like(nope), -rp[:, half:], rp[:, :half], pad]
    wq = jnp.concatenate(cols, axis=1)
    wqr = jnp.concatenate(rcols, axis=1)
    wq = jnp.concatenate([wq, zrow(wq)], axis=0).astype(BF16)
    wqr = jnp.concatenate([wqr, zrow(wqr)], axis=0).astype(BF16)
    return wq, wqr


def _pack_w_kvup(w):
    dk = A_NOPE + A_V
    kc, vc = [], []
    for h in range(A_HEADS):
        kc += [w[:, h * dk:h * dk + A_NOPE], jnp.zeros((A_KV_LORA, LANES - A_NOPE), F32)]
        vc += [w[:, h * dk + A_NOPE:(h + 1) * dk]]
    return jnp.concatenate(kc, axis=1).astype(BF16), jnp.concatenate(vc, axis=1).astype(BF16)


def _rope_placement():
    e = np.zeros((LANES, A_HEADS * LANES), np.float32)
    for h in range(A_HEADS):
        for r in range(A_ROPE):
            e[r, h * LANES + A_NOPE + r] = 1.0
    return jnp.asarray(e, dtype=BF16)


def _pool_blockdiag(pw):
    out = jnp.zeros((POOL_WIDTH, POOL_WIDTH), F32)
    for g in range(len(POOL_WINDOWS)):
        out = out.at[g * POOL_GROUP:(g + 1) * POOL_GROUP, g * POOL_GROUP:(g + 1) * POOL_GROUP].set(pw[g])
    return out.astype(BF16)


class _Tiles:
    def __init__(self, t):
        prompt = t > 128
        self.rows = 1024 if prompt else 128
        self.in_tn = 1280
        self.prep = 512 if prompt else t
        self.kv = 512
        self.mla_q = 256 if prompt else t
        self.mla_k = 512 if prompt else 256
        self.dsa_q = 128 if prompt else t
        self.dsa_kc = 512 if prompt else 256
        self.dsa_kf = 512 if prompt else 256
        self.pool = 512 if prompt else t
        self.conv = 256 if prompt else t
        self.conv_sub = 64 if prompt else t
        self.merge = 512 if prompt else t
        self.ffn = 1024 if prompt else t
        self.s_align = 512 if prompt else 256


def _trunk_layer(x3, caches, lw, consts, past):
    (w_in, gq, gkv, wq, wqr, wkn, wv, pool_bd, pool_scale, conv_w, conv_b, conv_g, conv_beta,
     w_branch, w_out, ln1_g, ln1_b, w_up, ffn_cw, ffn_cb, w_down, ln2_g, ln2_b) = lw
    tabs, bias, e_place = consts
    b, t, _ = x3.shape
    tl = _Tiles(t)
    s_valid = past + t
    s_pad = -(-s_valid // tl.s_align) * tl.s_align
    topk = min(TOPK_MAX, s_valid // 4)

    proj = _inproj(x3.reshape(b * t, D_MODEL), w_in, tl.rows, tl.in_tn)
    proj3 = proj.reshape(b, t, PROJ_W)
    qa, ckv_n, kr128 = _aprep(proj3, tabs, gq, gkv, wq, wqr, tl.prep)

    col = lambda blk, off, n: proj3[:, :, blk * 256 + off:blk * 256 + off + n]
    bk_new = col(CB_BK, 0, 256)
    bv_new = col(CB_BV, 0, 256)
    kidx_new = col(CB_IDX, 128, IDX_DIM)
    krope_new = kr128[:, :, :A_ROPE]

    def keys(cache, new, width):
        parts = [new] if cache is None else [cache.reshape(b, past, -1), new]
        if s_pad > s_valid:
            parts.append(jnp.zeros((b, s_pad - s_valid, width), new.dtype))
        return parts[0] if len(parts) == 1 else jnp.concatenate(parts, axis=1)

    c_ckv, c_krope, c_bk, c_bv, c_kidx, st_pool, st_conv, st_ffn = caches
    ckv_all = keys(c_ckv, ckv_n, A_KV_LORA)
    if c_krope is None:
        kr_all = keys(None, kr128, LANES)
    else:
        c_kr128 = jnp.pad(c_krope, ((0, 0), (0, 0), (0, LANES - A_ROPE)))
        kr_all = keys(c_kr128, kr128, LANES)
    bk_all = keys(c_bk, bk_new, 256).astype(BF16)
    bv_all = keys(c_bv, bv_new, 256).astype(BF16)
    ki_all = jnp.tile(keys(c_kidx, kidx_new, IDX_DIM).astype(BF16), (1, 1, IDX_HEADS))

    kx, vx = _kvup(ckv_all.reshape(b * s_pad, A_KV_LORA), kr_all.reshape(b * s_pad, LANES),
                   wkn, wv, e_place, tl.kv)
    out_a = _mla(qa, kx.reshape(b, s_pad, 512), vx.reshape(b, s_pad, 256),
                 tl.mla_q, tl.mla_k, past, s_valid)
    out_b = _dsa(proj3, bk_all, bv_all, ki_all, bias, tl.dsa_q, tl.dsa_kc, tl.dsa_kf,
                 past, s_valid, topk)
    out_c, new_pool = _pool(proj3, st_pool, pool_bd, pool_scale, tl.pool, past)
    out_d, new_conv = _conv(proj3, st_conv, conv_w, conv_b, conv_g, conv_beta, tl.conv, tl.conv_sub)
    x1 = _merge(out_a, out_b, out_c, out_d, proj3, x3, w_branch, w_out, ln1_g, ln1_b, tl.merge)
    x2, nst_v, nst_g = _ffn(x1, st_ffn, w_up, ffn_cw, ffn_cb, w_down, ln2_g, ln2_b, tl.ffn)
    new_ffn = jnp.concatenate([nst_v, nst_g], axis=-1)
    states = (ckv_n, krope_new, bk_new.reshape(b, t, B_HEADS, B_DIM),
              bv_new.reshape(b, t, B_HEADS, B_DIM), kidx_new, new_pool, new_conv, new_ffn)
    return x2, states


def kernel(x_prompt, x_sample, cache_a_ckv, cache_a_krope, cache_b_k, cache_b_v, cache_b_kidx,
           state_pool, state_conv, state_ffn, rel_bias, ln_in_g, ln_in_b, w_in, a_q_norm, a_kv_norm,
           a_w_qup, a_w_kvup, pool_w, pool_scale, conv_w, conv_b, conv_ln_g, conv_ln_b, w_branch,
           w_out, ln1_g, ln1_b, w_up, ffn_conv_w, ffn_conv_b, w_down, ln2_g, ln2_b):
    bp, tp, _ = x_prompt.shape
    bs, ts, _ = x_sample.shape
    past = cache_a_ckv.shape[2]
    depth = w_in.shape[0]

    xp = _layer_norm(x_prompt.reshape(bp * tp, D_MODEL), ln_in_g, ln_in_b,
                     _Tiles(tp).rows).reshape(bp, tp, D_MODEL)
    xs = _layer_norm(x_sample.reshape(bs * ts, D_MODEL), ln_in_g, ln_in_b,
                     _Tiles(ts).rows).reshape(bs, ts, D_MODEL)

    e_place = _rope_placement()
    consts_p = (_rope_tables(0, tp), _bias_tiles(rel_bias, _Tiles(tp).dsa_q), e_place)
    consts_s = (_rope_tables(past, ts), _bias_tiles(rel_bias, _Tiles(ts).dsa_q), e_place)
    zeros_p = (None, None, None, None, None,
               jnp.zeros((bp, POOL_STATE, POOL_WIDTH), F32),
               jnp.zeros((bp, CONV_K - 1, CONV_WIDTH), F32),
               jnp.zeros((bp, FFN_K - 1, 2 * D_FF), F32))

    row = lambda v: v.reshape(1, -1)
    p_states, s_states = [], []
    for l in range(depth):
        wq, wqr = _pack_w_qup(a_w_qup[l])
        wkn, wv = _pack_w_kvup(a_w_kvup[l])
        gq = jnp.concatenate([a_q_norm[l], jnp.zeros((256 - A_Q_LORA,), F32)]).reshape(1, 256)
        lw = (_pack_w_in(w_in[l]), gq, row(a_kv_norm[l]), wq, wqr, wkn, wv,
              _pool_blockdiag(pool_w[l]), row(pool_scale[l]), conv_w[l], row(conv_b[l]),
              row(conv_ln_g[l]), row(conv_ln_b[l]), w_branch[l].astype(BF16), w_out[l].astype(BF16),
              row(ln1_g[l]), row(ln1_b[l]), w_up[l].astype(BF16), ffn_conv_w[l], row(ffn_conv_b[l]),
              w_down[l].astype(BF16), row(ln2_g[l]), row(ln2_b[l]))
        xp, st_p = _trunk_layer(xp, zeros_p, lw, consts_p, 0)
        caches_s = (cache_a_ckv[l], cache_a_krope[l], cache_b_k[l], cache_b_v[l], cache_b_kidx[l],
                    state_pool[l], state_conv[l], state_ffn[l])
        xs, st_s = _trunk_layer(xs, caches_s, lw, consts_s, past)
        p_states.append(st_p)
        s_states.append(st_s)

    p_out = [jnp.stack(a) for a in zip(*p_states)]
    s_out = [jnp.stack(a) for a in zip(*s_states)]
    return (xp, xs, *p_out, *s_out)
```

```python
import functools

import numpy as np
import jax
import jax.numpy as jnp
from jax import lax
from jax.experimental import pallas as pl
from jax.experimental.pallas import tpu as pltpu

F32 = jnp.float32
BF16 = jnp.bfloat16

D_MODEL = 1024
DEPTH = 4
CHUNK = 64
CHUNK_SHIFT = 6
N_BRANCH = 4
A_HEADS = 4
A_NOPE = 64
A_ROPE = 32
A_V = 64
A_Q_LORA = 192
A_KV_LORA = 128
A_SCALE = (A_NOPE + A_ROPE) ** -0.5
ROPE_BASE = 10000.0
B_HEADS = 4
B_DIM = 64
B_SCALE = B_DIM ** -0.5
IDX_HEADS = 4
IDX_DIM = 32
IDX_SCALE = (IDX_HEADS ** -0.5) * (IDX_DIM ** -0.5)
TOPK_MAX = 256
REL_BUCKETS = 32
POOL_WINDOWS = (2, 4, 8, 16)
POOL_WIDTH = 256
POOL_GROUP = 64
POOL_STATE = 15
CONV_WIDTH = 256
CONV_K = 31
D_FF = 2816
FFN_K = 3
ALPHA = (2 * DEPTH) ** 0.25
LN_EPS = 1e-5

IN_SPLITS = (A_Q_LORA, A_KV_LORA, A_ROPE, 256, 256, 256, 128, IDX_DIM, IDX_HEADS,
             POOL_WIDTH, 2 * CONV_WIDTH, N_BRANCH * D_MODEL)
_OFF = [0] + [int(v) for v in np.cumsum(IN_SPLITS)]
(O_CQ, O_CKV, O_KROPE, O_BQ, O_BK, O_BV, O_QIDX, O_KIDX, O_WIDX, O_POOL, O_CONV, O_GATES,
 D_IN) = _OFF

PROJ_W = 6400
CB_GATES = 0
CB_CQ = 16
CB_CK = 17
CB_BQ = 18
CB_BK = 19
CB_BV = 20
CB_IDX = 21
CB_CONV = 11
CB_POOL = 24

VMEM_LIMIT_BYTES = 48 * 1024 * 1024
LANES = 128

NEG = -1e30
M_INIT = -5e29
SEARCH_MAX_IT = 320
SNAP_EVERY = 4

_NT = (((1,), (1,)), ((), ()))


def _cparams(sem):
    return pltpu.CompilerParams(dimension_semantics=sem, vmem_limit_bytes=VMEM_LIMIT_BYTES)


def _ln_rows(x, g, b):
    mu = jnp.mean(x, axis=-1, keepdims=True)
    xc = x - mu
    var = jnp.mean(xc * xc, axis=-1, keepdims=True)
    return xc * lax.rsqrt(var + LN_EPS) * g + b


def _sigmoid(x):
    return 1.0 / (1.0 + jnp.exp(-x))


def _ln_kernel(x_ref, g_ref, b_ref, o_ref):
    o_ref[...] = _ln_rows(x_ref[...], g_ref[...], b_ref[...])


def _layer_norm(x2d, g, b, tm):
    n, d = x2d.shape
    return pl.pallas_call(
        _ln_kernel,
        out_shape=jax.ShapeDtypeStruct((n, d), F32),
        grid=(n // tm,),
        in_specs=[pl.BlockSpec((tm, d), lambda i: (i, 0)),
                  pl.BlockSpec((1, d), lambda i: (0, 0)),
                  pl.BlockSpec((1, d), lambda i: (0, 0))],
        out_specs=pl.BlockSpec((tm, d), lambda i: (i, 0)),
        compiler_params=_cparams(("parallel",)),
        name="input_ln",
    )(x2d, g.reshape(1, d), b.reshape(1, d))


def _inproj_kernel(x_ref, w_ref, o_ref, xb_ref):
    @pl.when(pl.program_id(1) == 0)
    def _():
        xb_ref[...] = x_ref[...].astype(BF16)

    o_ref[...] = jnp.dot(xb_ref[...], w_ref[...], preferred_element_type=F32)


def _inproj(x2d, w, tm, tn):
    n, k = x2d.shape
    nw = w.shape[1]
    return pl.pallas_call(
        _inproj_kernel,
        out_shape=jax.ShapeDtypeStruct((n, nw), F32),
        grid=(n // tm, nw // tn),
        in_specs=[pl.BlockSpec((tm, k), lambda i, j: (i, 0)),
                  pl.BlockSpec((k, tn), lambda i, j: (0, j))],
        out_specs=pl.BlockSpec((tm, tn), lambda i, j: (i, j)),
        scratch_shapes=[pltpu.VMEM((tm, k), BF16)],
        compiler_params=_cparams(("parallel", "arbitrary")),
        name="in_proj",
    )(x2d, w)


def _aprep_kernel(cq_ref, ck_ref, cosq_ref, sinq_ref, cosk_ref, sink_ref, gq_ref, gkv_ref,
                  wq_ref, wqr_ref, qa_ref, ckv_ref, kr_ref):
    cq = cq_ref[...]
    ms = jnp.sum(cq * cq, axis=-1, keepdims=True) * (1.0 / A_Q_LORA)
    cqb = ((cq * lax.rsqrt(ms + LN_EPS)) * gq_ref[...]).astype(BF16)
    q = jnp.dot(cqb, wq_ref[...], preferred_element_type=F32)
    qr = jnp.dot(cqb, wqr_ref[...], preferred_element_type=F32)
    cos = jnp.concatenate([cosq_ref[...]] * A_HEADS, axis=1)
    sin = jnp.concatenate([sinq_ref[...]] * A_HEADS, axis=1)
    qa_ref[...] = ((q * cos + qr * sin) * A_SCALE).astype(BF16)

    ck = ck_ref[...]
    ckv = ck[:, :A_KV_LORA]
    ms2 = jnp.mean(ckv * ckv, axis=-1, keepdims=True)
    ckv_ref[...] = (ckv * lax.rsqrt(ms2 + LN_EPS)) * gkv_ref[...]
    kr = ck[:, A_KV_LORA:]
    krr = pltpu.roll(kr, LANES - A_ROPE, 1)
    kr_ref[...] = kr * cosk_ref[...] + krr * sink_ref[...]


def _aprep(proj3, tabs, gq, gkv, wq, wqr, tm):
    b, t, _ = proj3.shape
    cosq, sinq, cosk, sink = tabs
    tab_spec = pl.BlockSpec((tm, LANES), lambda bi, i: (i, 0))
    full = lambda shape: pl.BlockSpec(shape, lambda bi, i: (0,) * len(shape))
    return pl.pallas_call(
        _aprep_kernel,
        out_shape=(jax.ShapeDtypeStruct((b, t, 512), BF16),
                   jax.ShapeDtypeStruct((b, t, A_KV_LORA), F32),
                   jax.ShapeDtypeStruct((b, t, LANES), F32)),
        grid=(b, t // tm),
        in_specs=[pl.BlockSpec((None, tm, 256), lambda bi, i: (bi, i, CB_CQ)),
                  pl.BlockSpec((None, tm, 256), lambda bi, i: (bi, i, CB_CK)),
                  tab_spec, tab_spec, tab_spec, tab_spec,
                  full((1, 256)), full((1, A_KV_LORA)), full((256, 512)), full((256, 512))],
        out_specs=(pl.BlockSpec((None, tm, 512), lambda bi, i: (bi, i, 0)),
                   pl.BlockSpec((None, tm, A_KV_LORA), lambda bi, i: (bi, i, 0)),
                   pl.BlockSpec((None, tm, LANES), lambda bi, i: (bi, i, 0))),
        compiler_params=_cparams(("parallel", "parallel")),
        name="mla_prep",
    )(proj3, proj3, cosq, sinq, cosk, sink, gq, gkv, wq, wqr)


def _kvup_kernel(ckv_ref, kr_ref, wkn_ref, wv_ref, e_ref, k_ref, v_ref):
    cb = ckv_ref[...].astype(BF16)
    k = jnp.dot(cb, wkn_ref[...], preferred_element_type=F32)
    k = k + jnp.dot(kr_ref[...].astype(BF16), e_ref[...], preferred_element_type=F32)
    k_ref[...] = k.astype(BF16)
    v = jnp.dot(cb, wv_ref[...], preferred_element_type=F32).astype(BF16)
    for h in range(A_HEADS):
        v_ref[h] = v[:, h * A_V:(h + 1) * A_V]


def _kvup(ckv2d, kr2d, wkn, wv, e, tm):
    m = ckv2d.shape[0]
    full = lambda shape: pl.BlockSpec(shape, lambda i: (0,) * len(shape))
    return pl.pallas_call(
        _kvup_kernel,
        out_shape=(jax.ShapeDtypeStruct((m, 512), BF16),
                   jax.ShapeDtypeStruct((A_HEADS, m, A_V), BF16)),
        grid=(m // tm,),
        in_specs=[pl.BlockSpec((tm, A_KV_LORA), lambda i: (i, 0)),
                  pl.BlockSpec((tm, LANES), lambda i: (i, 0)),
                  full((A_KV_LORA, 512)), full((A_KV_LORA, 256)), full((LANES, 512))],
        out_specs=(pl.BlockSpec((tm, 512), lambda i: (i, 0)),
                   pl.BlockSpec((A_HEADS, tm, A_V), lambda i: (0, i, 0))),
        compiler_params=_cparams(("parallel",)),
        name="kv_up",
    )(ckv2d, kr2d, wkn, wv, e)


def _online(carry, s, v):
    m, l, acc = carry
    m_new = jnp.maximum(m, jnp.max(s, axis=1, keepdims=True))
    alpha = jnp.exp(m - m_new)
    p = jnp.exp(s - m_new)
    l = alpha * l + jnp.sum(p, axis=1, keepdims=True)
    acc = alpha * acc + jnp.dot(p.astype(BF16), v, preferred_element_type=F32)
    return m_new, l, acc


def _softmax_init(tq, dv):
    return (jnp.full((tq, 1), M_INIT, F32), jnp.zeros((tq, 1), F32), jnp.zeros((tq, dv), F32))


def _visible_limit(qpos, s_valid):
    return jnp.minimum((jnp.right_shift(qpos, CHUNK_SHIFT) + 1) * CHUNK, s_valid)


def _mla_kernel(q_ref, k_ref, v_ref, o_ref, *, tq, tk, past, s_valid):
    i = pl.program_id(1)
    q0 = past + i * tq
    qpos = q0 + lax.broadcasted_iota(jnp.int32, (tq, 1), 0)
    q_lim = _visible_limit(qpos, s_valid)
    lim_first = _visible_limit(q0, s_valid)
    lim_last = _visible_limit(q0 + tq - 1, s_valid)
    n_full = lim_first // tk
    n_tot = (lim_last + tk - 1) // tk
    q = q_ref[...]
    qhs = [q[:, h * LANES:(h + 1) * LANES] for h in range(A_HEADS)]

    def body(j, carries, masked):
        ks = pl.ds(pl.multiple_of(j * tk, tk), tk)
        if masked:
            kpos = j * tk + lax.broadcasted_iota(jnp.int32, (1, tk), 1)
            vis = kpos < q_lim
        out = []
        for h in range(A_HEADS):
            s = lax.dot_general(qhs[h], k_ref[ks, h * LANES:(h + 1) * LANES], _NT,
                                preferred_element_type=F32)
            if masked:
                s = jnp.where(vis, s, NEG)
            out.append(_online(carries[h], s, v_ref[h, ks, :]))
        return tuple(out)

    init = tuple(_softmax_init(tq, A_V) for _ in range(A_HEADS))
    carries = lax.fori_loop(0, n_full, functools.partial(body, masked=False), init)
    carries = lax.fori_loop(n_full, n_tot, functools.partial(body, masked=True), carries)
    for h in range(A_HEADS):
        _, l, acc = carries[h]
        o_ref[:, h * A_V:(h + 1) * A_V] = acc / l


def _mla(qa, kx, v, tq, tk, past, s_valid):
    b, t, _ = qa.shape
    s_pad = kx.shape[1]
    kern = functools.partial(_mla_kernel, tq=tq, tk=tk, past=past, s_valid=s_valid)
    return pl.pallas_call(
        kern,
        out_shape=jax.ShapeDtypeStruct((b, t, 256), F32),
        grid=(b, t // tq),
        in_specs=[pl.BlockSpec((None, tq, 512), lambda bi, i: (bi, i, 0)),
                  pl.BlockSpec((None, s_pad, 512), lambda bi, i: (bi, 0, 0)),
                  pl.BlockSpec((A_HEADS, None, s_pad, A_V), lambda bi, i: (0, bi, 0, 0))],
        out_specs=pl.BlockSpec((None, tq, 256), lambda bi, i: (bi, i, 0)),
        compiler_params=_cparams(("parallel", "arbitrary")),
        name="mla_attn",
    )(qa, kx, v)


KEY_NONE = -2 ** 31
KEY_MAX = 2 ** 31 - 1


def _order_key(x):
    b = lax.bitcast_convert_type(x, jnp.int32)
    return b ^ (jnp.right_shift(b, 31) & KEY_MAX)


def _order_key_inv(k):
    return lax.bitcast_convert_type(k ^ (jnp.right_shift(k, 31) & KEY_MAX), F32)


def _dsa_kernel(bq_ref, idx_ref, bk_ref, bv_ref, ki_ref, bias_ref, o_ref, sc_ref, sct_ref, msk_ref,
                *, tq, tqs, tkc, past, s_valid, topk):
    i = pl.program_id(1)
    q0 = past + i * tq
    qpos = q0 + lax.broadcasted_iota(jnp.int32, (tq, 1), 0)
    q_lim = _visible_limit(qpos, s_valid)
    lim_last = _visible_limit(q0 + tq - 1, s_valid)
    nch = (lim_last + tkc - 1) // tkc
    ngrp = tkc // LANES
    kf = float(topk)

    def chunk(c):
        return pl.ds(pl.multiple_of(c * tkc, tkc), tkc)

    def key_pos(c):
        return c * tkc + lax.broadcasted_iota(jnp.int32, (1, tkc), 1)

    def key_pos_col(c):
        return (c * tkc + lax.broadcasted_iota(jnp.int32, (tkc, 1), 0)).astype(F32)

    idx = idx_ref[...]
    lane = lax.broadcasted_iota(jnp.int32, (1, LANES), 1)
    qi = idx[:, :LANES]
    qhs = [jnp.where((lane >= h * IDX_DIM) & (lane < (h + 1) * IDX_DIM), qi, 0.0).astype(BF16)
           for h in range(IDX_HEADS)]
    wi = idx[:, LANES + IDX_DIM:LANES + IDX_DIM + IDX_HEADS] * IDX_SCALE
    whs = [wi[:, h:h + 1] for h in range(IDX_HEADS)]

    def score_chunk(c, carry):
        ki = ki_ref[chunk(c), :]
        tot = None
        for h in range(IDX_HEADS):
            d = lax.dot_general(qhs[h], ki, _NT, preferred_element_type=F32)
            term = whs[h] * jnp.maximum(d, 0.0)
            tot = term if tot is None else tot + term
        sc = jnp.where(key_pos(c) < q_lim, _order_key(tot), KEY_NONE)
        sc_ref[:, chunk(c)] = sc
        if tqs > tq:
            sc = jnp.concatenate([sc, jnp.full((tqs - tq, tkc), KEY_NONE, jnp.int32)], axis=0)
        for g in range(ngrp):
            rows = pl.ds(pl.multiple_of(c * tkc + g * LANES, LANES), LANES)
            sct_ref[rows, :] = sc[:, g * LANES:(g + 1) * LANES].T
        return carry

    lax.fori_loop(0, nch, score_chunk, 0)

    def fold8(x, op):
        pair = {jnp.sum: jnp.add, jnp.min: jnp.minimum, jnp.max: jnp.maximum}[op]
        parts = [x[8 * r:8 * (r + 1)] for r in range(tkc // 8)]
        while len(parts) > 1:
            parts = [pair(parts[r], parts[r + 1]) for r in range(0, len(parts), 2)]
        return parts[0]

    def count_where(pred):
        def body(c, acc):
            m = jnp.where(pred(sct_ref[chunk(c), :], c), 1.0, 0.0)
            return acc + fold8(m, jnp.sum)
        acc = lax.fori_loop(0, nch, body, jnp.zeros((8, tqs), F32))
        return jnp.sum(acc, axis=0, keepdims=True)

    def count_gt(t):
        return count_where(lambda x, c: x > t)

    def search():
        lane_q = lax.broadcasted_iota(jnp.int32, (1, tqs), 1)
        nvis = jnp.where(lane_q < tq, _visible_limit(q0 + lane_q, s_valid), 0).astype(F32)

        def mm_body(c, carry):
            mn, mx = carry
            x = sct_ref[chunk(c), :]
            xm = jnp.where(x == KEY_NONE, KEY_MAX, x)
            return (jnp.minimum(mn, fold8(xm, jnp.min)), jnp.maximum(mx, fold8(x, jnp.max)))
        mn, mx = lax.fori_loop(0, nch, mm_body, (jnp.full((8, tqs), KEY_MAX, jnp.int32),
                                                 jnp.full((8, tqs), KEY_NONE, jnp.int32)))
        mn = jnp.min(mn, axis=0, keepdims=True)
        mx = jnp.max(mx, axis=0, keepdims=True)
        done0 = jnp.where(nvis <= kf, 1.0, 0.0)
        mn = jnp.where(nvis > 0.5, mn, 0)
        mx = jnp.where(nvis > 0.5, mx, 0)
        lo0 = mn - 1
        zeros = jnp.zeros((1, tqs), F32)
        izeros = jnp.zeros((1, tqs), jnp.int32)

        def snap(lo, hi, done, tie, vt):
            def body(c, carry):
                a, b = carry
                x = sct_ref[chunk(c), :]
                xa = jnp.where(x > lo, x, KEY_MAX)
                xb = jnp.where(x <= hi, x, KEY_NONE)
                return (jnp.minimum(a, fold8(xa, jnp.min)), jnp.maximum(b, fold8(xb, jnp.max)))
            a, b = lax.fori_loop(0, nch, body, (jnp.full((8, tqs), KEY_MAX, jnp.int32),
                                                jnp.full((8, tqs), KEY_NONE, jnp.int32)))
            a = jnp.min(a, axis=0, keepdims=True)
            b = jnp.max(b, axis=0, keepdims=True)
            nd = done < 0.5
            new_tie = nd & (a >= b)
            hi = jnp.where(nd & (a < b), b, hi)
            vt = jnp.where(new_tie, b, vt)
            tie = jnp.where(new_tie, 1.0, tie)
            done = jnp.where(new_tie, 1.0, done)
            return lo, hi, done, tie, vt

        def cond(s):
            return (s[1] > 0) & (s[0] < SEARCH_MAX_IT)

        def body(s):
            it, _, lo, hi, clo, chi, t, done, tie, vt = s
            stuck = hi <= lo + 1
            lo_f, hi_f = _order_key_inv(lo), _order_key_inv(hi)
            itp = lo_f + (hi_f - lo_f) * ((clo - kf) / (clo - chi))
            cand = _order_key(jnp.where(it % 2 == 1, itp, 0.5 * lo_f + 0.5 * hi_f))
            imid = (lo & hi) + jnp.right_shift(lo ^ hi, 1)
            mid = jnp.where((cand > lo) & (cand < hi), cand, imid)
            c = count_gt(mid)
            nd = done < 0.5
            live = nd & jnp.logical_not(stuck)
            hit = live & (c == kf)
            stk = nd & stuck
            up = live & (c > kf)
            dn = live & (c < kf)
            t = jnp.where(hit, mid, t)
            vt = jnp.where(stk, hi, vt)
            tie = jnp.where(stk, 1.0, tie)
            lo = jnp.where(up, mid, lo)
            clo = jnp.where(up, c, clo)
            hi = jnp.where(dn, mid, hi)
            chi = jnp.where(dn, c, chi)
            done = jnp.where(hit | stk, 1.0, done)

            lo, hi, done, tie, vt = lax.cond(
                it % SNAP_EVERY == SNAP_EVERY - 1, snap,
                lambda *a: a, lo, hi, done, tie, vt)
            active = (jnp.max(1.0 - done) > 0.5).astype(jnp.int32)
            return it + 1, active, lo, hi, clo, chi, t, done, tie, vt

        active0 = (jnp.max(1.0 - done0) > 0.5).astype(jnp.int32)
        s = lax.while_loop(cond, body, (jnp.int32(0), active0, lo0, mx, nvis, zeros, lo0, done0,
                                        zeros, izeros))
        t, tie, vt = s[6], s[8], s[9]

        def tie_cut():
            t2 = jnp.where(tie > 0.5, vt, t)
            need = kf - count_gt(t2)

            def bs_body(_, carry):
                lo_c, hi_c = carry
                mid_c = jnp.floor((lo_c + hi_c) * 0.5)
                cnt = count_where(lambda x, c: (x == vt) & (key_pos_col(c) < mid_c))
                ok = cnt >= need
                return jnp.where(ok, lo_c, mid_c), jnp.where(ok, mid_c, hi_c)

            nbits = int(s_valid).bit_length()
            _, cut = lax.fori_loop(0, nbits, bs_body,
                                   (zeros, jnp.full((1, tqs), float(1 << nbits), F32)))
            return t2, jnp.where(tie > 0.5, cut, 0.0)

        any_tie = jnp.max(tie) > 0.5
        t, cut = lax.cond(any_tie, tie_cut, lambda: (t, zeros))
        return t, tie, vt, cut

    def no_search():
        z = jnp.zeros((1, tqs), F32)
        return (jnp.full((1, tqs), KEY_NONE, jnp.int32), z, jnp.zeros((1, tqs), jnp.int32), z)

    t, tie, vt, cut = lax.cond(lim_last > topk, search, no_search)

    def to_col(v):
        return jnp.broadcast_to(v, (tqs, tqs)).T[:tq, :LANES]

    t, vt, cut = to_col(t), to_col(vt), to_col(cut)
    tied = to_col(tie) > 0.5
    lane_f = lax.broadcasted_iota(jnp.int32, (1, LANES), 1).astype(F32)

    def to_mask(c, carry):
        for g in range(ngrp):
            cols = pl.ds(pl.multiple_of(c * tkc + g * LANES, LANES), LANES)
            x = sc_ref[:, cols]
            kpos = (c * tkc + g * LANES).astype(F32) + lane_f
            sel = (x > t) | (tied & (x == vt) & (kpos < cut))
            msk_ref[:, cols] = jnp.where(sel, 0.0, NEG)
        return carry

    lax.fori_loop(0, nch, to_mask, 0)

    bq = (bq_ref[...] * B_SCALE).astype(BF16)
    qbs = [bq[:, h * B_DIM:(h + 1) * B_DIM] for h in range(B_HEADS)]
    d0 = q0 // LANES

    def att_body(c, carries):
        ks = chunk(c)
        msk = msk_ref[:, ks]
        tiles = [jnp.clip(d0 - (c * ngrp + g), 0, 2) for g in range(ngrp)]
        out = []
        for h in range(B_HEADS):
            s = lax.dot_general(qbs[h], bk_ref[h, ks, :], _NT, preferred_element_type=F32)
            bias = jnp.concatenate([bias_ref[tl, h] for tl in tiles], axis=1)
            out.append(_online(carries[h], s + (msk + bias), bv_ref[h, ks, :]))
        return tuple(out)

    init = tuple(_softmax_init(tq, B_DIM) for _ in range(B_HEADS))
    carries = lax.fori_loop(0, nch, att_body, init)
    for h in range(B_HEADS):
        _, l, acc = carries[h]
        o_ref[:, h * B_DIM:(h + 1) * B_DIM] = acc / l


def _dsa(proj3, bk, bv, ki, bias, tq, tkc, past, s_valid, topk):
    b, t, _ = proj3.shape
    s_pad = bk.shape[2]
    tqs = max(tq, LANES)
    assert past % LANES == 0 and (tq == LANES or t == tq) and s_pad % tkc == 0
    kern = functools.partial(_dsa_kernel, tq=tq, tqs=tqs, tkc=tkc, past=past, s_valid=s_valid,
                             topk=topk)
    return pl.pallas_call(
        kern,
        out_shape=jax.ShapeDtypeStruct((b, t, 256), F32),
        grid=(b, t // tq),
        in_specs=[pl.BlockSpec((None, tq, 256), lambda bi, i: (bi, i, CB_BQ)),
                  pl.BlockSpec((None, tq, 256), lambda bi, i: (bi, i, CB_IDX)),
                  pl.BlockSpec((None, B_HEADS, s_pad, B_DIM), lambda bi, i: (bi, 0, 0, 0)),
                  pl.BlockSpec((None, B_HEADS, s_pad, B_DIM), lambda bi, i: (bi, 0, 0, 0)),
                  pl.BlockSpec((None, s_pad, LANES), lambda bi, i: (bi, 0, 0)),
                  pl.BlockSpec((3, B_HEADS, tq, LANES), lambda bi, i: (0, 0, 0, 0))],
        out_specs=pl.BlockSpec((None, tq, 256), lambda bi, i: (bi, i, 0)),
        scratch_shapes=[pltpu.VMEM((tq, s_pad), jnp.int32), pltpu.VMEM((s_pad, tqs), jnp.int32),
                        pltpu.VMEM((tq, s_pad), F32)],
        compiler_params=_cparams(("parallel", "arbitrary")),
        name="dsa_attn",
    )(proj3, proj3, bk, bv, ki, bias)


def _pool_kernel(u_ref, prev_ref, w_ref, scale_ref, o_ref, st_ref, ext_ref, *, tr, past):
    i = pl.program_id(1)

    @pl.when(i == 0)
    def _():
        ext_ref[1:16, :] = prev_ref[...]

    u = u_ref[...]
    ext_ref[16:16 + tr, :] = u

    def tap(k, lo):
        return ext_ref[16 - k:16 - k + tr, lo:lo + LANES]

    pos1 = past + i * tr + 1 + lax.broadcasted_iota(jnp.int32, (tr, 1), 0)
    cnt = [jnp.minimum(w, pos1).astype(F32) for w in POOL_WINDOWS]
    lane = lax.broadcasted_iota(jnp.int32, (1, LANES), 1)
    first = lane < POOL_GROUP

    s2 = tap(0, 0) + tap(1, 0)
    s4 = s2 + tap(2, 0) + tap(3, 0)
    s8 = tap(0, LANES)
    for k in range(1, 8):
        s8 = s8 + tap(k, LANES)
    s16 = s8
    for k in range(8, 16):
        s16 = s16 + tap(k, LANES)
    m_lo = jnp.where(first, s2 / cnt[0], s4 / cnt[1])
    m_hi = jnp.where(first, s8 / cnt[2], s16 / cnt[3])
    pooled = jnp.concatenate([m_lo, m_hi], axis=1) - u
    mixed = jnp.dot(pooled.astype(BF16), w_ref[...], preferred_element_type=F32)
    o_ref[...] = mixed * scale_ref[...]

    new_state = ext_ref[1 + tr:16 + tr, :]
    st_ref[...] = new_state
    ext_ref[1:16, :] = new_state


def _pool(proj3, prev, w_bd, scale, tr, past):
    b, t, _ = proj3.shape
    kern = functools.partial(_pool_kernel, tr=tr, past=past)
    return pl.pallas_call(
        kern,
        out_shape=(jax.ShapeDtypeStruct((b, t, POOL_WIDTH), F32),
                   jax.ShapeDtypeStruct((b, POOL_STATE, POOL_WIDTH), F32)),
        grid=(b, t // tr),
        in_specs=[pl.BlockSpec((None, tr, 256), lambda bi, i: (bi, i, CB_POOL)),
                  pl.BlockSpec((None, POOL_STATE, POOL_WIDTH), lambda bi, i: (bi, 0, 0)),
                  pl.BlockSpec((256, 256), lambda bi, i: (0, 0)),
                  pl.BlockSpec((1, 256), lambda bi, i: (0, 0))],
        out_specs=(pl.BlockSpec((None, tr, 256), lambda bi, i: (bi, i, 0)),
                   pl.BlockSpec((None, POOL_STATE, POOL_WIDTH), lambda bi, i: (bi, 0, 0))),
        scratch_shapes=[pltpu.VMEM((tr + 16, POOL_WIDTH), F32)],
        compiler_params=_cparams(("parallel", "arbitrary")),
        name="pool_mix",
    )(proj3, prev, w_bd, scale)


def _conv_kernel(u_ref, prev_ref, w_ref, b_ref, g_ref, beta_ref, o_ref, st_ref, ext_ref, *, tr, sub):
    i = pl.program_id(1)
    npre = CONV_K - 1

    @pl.when(i == 0)
    def _():
        ext_ref[2:2 + npre, :] = prev_ref[...]

    u = u_ref[...]
    ext_ref[32:32 + tr, :] = u[:, :CONV_WIDTH] * _sigmoid(u[:, CONV_WIDTH:])

    for r0 in range(0, tr, sub):
        acc = ext_ref[r0 + 2:r0 + 2 + sub, :] * w_ref[0:1, :]
        for k in range(1, CONV_K):
            acc = acc + ext_ref[r0 + 2 + k:r0 + 2 + k + sub, :] * w_ref[k:k + 1, :]
        y = _ln_rows(acc + b_ref[...], g_ref[...], beta_ref[...])
        o_ref[r0:r0 + sub, :] = y * _sigmoid(y)

    new_state = ext_ref[2 + tr:2 + tr + npre, :]
    st_ref[...] = new_state
    ext_ref[2:2 + npre, :] = new_state


def _conv(proj3, prev, w, bias, g, beta, tr, sub):
    b, t, _ = proj3.shape
    npre = CONV_K - 1
    kern = functools.partial(_conv_kernel, tr=tr, sub=sub)
    vec = pl.BlockSpec((1, CONV_WIDTH), lambda bi, i: (0, 0))
    return pl.pallas_call(
        kern,
        out_shape=(jax.ShapeDtypeStruct((b, t, CONV_WIDTH), F32),
                   jax.ShapeDtypeStruct((b, npre, CONV_WIDTH), F32)),
        grid=(b, t // tr),
        in_specs=[pl.BlockSpec((None, tr, 512), lambda bi, i: (bi, i, CB_CONV)),
                  pl.BlockSpec((None, npre, CONV_WIDTH), lambda bi, i: (bi, 0, 0)),
                  pl.BlockSpec((CONV_K, CONV_WIDTH), lambda bi, i: (0, 0)),
                  vec, vec, vec],
        out_specs=(pl.BlockSpec((None, tr, CONV_WIDTH), lambda bi, i: (bi, i, 0)),
                   pl.BlockSpec((None, npre, CONV_WIDTH), lambda bi, i: (bi, 0, 0))),
        scratch_shapes=[pltpu.VMEM((tr + 32, CONV_WIDTH), F32)],
        compiler_params=_cparams(("parallel", "arbitrary")),
        name="conv_module",
    )(proj3, prev, w, bias, g, beta)


def _merge_kernel(a_ref, b_ref, c_ref, d_ref, gates_ref, x_ref, wb_ref, wo_ref, g_ref, beta_ref,
                  o_ref):
    mixed = None
    for n, br_ref in enumerate((a_ref, b_ref, c_ref, d_ref)):
        br = jnp.dot(br_ref[...].astype(BF16), wb_ref[n], preferred_element_type=F32)
        term = _sigmoid(gates_ref[:, n * D_MODEL:(n + 1) * D_MODEL]) * br
        mixed = term if mixed is None else mixed + term
    y = jnp.dot(mixed.astype(BF16), wo_ref[...], preferred_element_type=F32)
    o_ref[...] = _ln_rows(ALPHA * x_ref[...] + y, g_ref[...], beta_ref[...])


def _merge(oa, ob, oc, od, proj3, x3, wb, wo, g, beta, tm):
    b, t, _ = x3.shape
    br_spec = pl.BlockSpec((None, tm, 256), lambda bi, i: (bi, i, 0))
    vec = pl.BlockSpec((1, D_MODEL), lambda bi, i: (0, 0))
    return pl.pallas_call(
        _merge_kernel,
        out_shape=jax.ShapeDtypeStruct((b, t, D_MODEL), F32),
        grid=(b, t // tm),
        in_specs=[br_spec, br_spec, br_spec, br_spec,
                  pl.BlockSpec((None, tm, N_BRANCH * D_MODEL), lambda bi, i: (bi, i, CB_GATES)),
                  pl.BlockSpec((None, tm, D_MODEL), lambda bi, i: (bi, i, 0)),
                  pl.BlockSpec((N_BRANCH, 256, D_MODEL), lambda bi, i: (0, 0, 0)),
                  pl.BlockSpec((D_MODEL, D_MODEL), lambda bi, i: (0, 0)),
                  vec, vec],
        out_specs=pl.BlockSpec((None, tm, D_MODEL), lambda bi, i: (bi, i, 0)),
        compiler_params=_cparams(("parallel", "parallel")),
        name="merge_ln1",
    )(oa, ob, oc, od, proj3, x3, wb, wo, g, beta)


FF_TC = 256
FF_NC = D_FF // FF_TC


def _ffn_kernel(x_ref, stv_ref, stg_ref, wv_ref, wg_ref, cwv_ref, cwg_ref, cbv_ref, cbg_ref,
                wd_ref, g_ref, beta_ref, o_ref, nstv_ref, nstg_ref,
                xb_ref, acc_ref, carv_ref, carg_ref, ubuf_ref, *, tm):
    i = pl.program_id(1)
    c = pl.program_id(2)

    @pl.when(c == 0)
    def _():
        xb_ref[...] = x_ref[...].astype(BF16)
        acc_ref[...] = jnp.zeros_like(acc_ref)

    @pl.when(i == 0)
    def _():
        carv_ref[c, 6:8, :] = stv_ref[...]
        carg_ref[c, 6:8, :] = stg_ref[...]

    xb = xb_ref[...]

    def conv3(w_ref, cw_ref, cb_ref, car_ref, nst_ref):
        u = jnp.dot(xb, w_ref[...], preferred_element_type=F32)
        ubuf_ref[6:8, :] = car_ref[c, 6:8, :]
        ubuf_ref[8:8 + tm, :] = u
        y = (cw_ref[0:1, :] * ubuf_ref[6:6 + tm, :] + cw_ref[1:2, :] * ubuf_ref[7:7 + tm, :]
             + cw_ref[2:3, :] * u + cb_ref[...])
        last = ubuf_ref[6 + tm:8 + tm, :]
        car_ref[c, 6:8, :] = last
        nst_ref[:, pl.ds(pl.multiple_of(c * FF_TC, FF_TC), FF_TC)] = last
        return y

    val = conv3(wv_ref, cwv_ref, cbv_ref, carv_ref, nstv_ref)
    gate = conv3(wg_ref, cwg_ref, cbg_ref, carg_ref, nstg_ref)
    h = (gate * _sigmoid(gate)) * val
    acc_ref[...] += jnp.dot(h.astype(BF16), wd_ref[...], preferred_element_type=F32)

    @pl.when(c == FF_NC - 1)
    def _():
        o_ref[...] = _ln_rows(ALPHA * x_ref[...] + acc_ref[...], g_ref[...], beta_ref[...])


def _ffn(x3, st, w_up, cw, cb, w_dn, g, beta, tm):
    b, t, _ = x3.shape
    kern = functools.partial(_ffn_kernel, tm=tm)
    nk = FFN_K - 1
    vec = pl.BlockSpec((1, D_MODEL), lambda bi, i, c: (0, 0))
    st_v = pl.BlockSpec((None, nk, FF_TC), lambda bi, i, c: (bi, 0, c))
    st_g = pl.BlockSpec((None, nk, FF_TC), lambda bi, i, c: (bi, 0, FF_NC + c))
    return pl.pallas_call(
        kern,
        out_shape=(jax.ShapeDtypeStruct((b, t, D_MODEL), F32),
                   jax.ShapeDtypeStruct((b, nk, D_FF), F32),
                   jax.ShapeDtypeStruct((b, nk, D_FF), F32)),
        grid=(b, t // tm, FF_NC),
        in_specs=[pl.BlockSpec((None, tm, D_MODEL), lambda bi, i, c: (bi, i, 0)),
                  st_v, st_g,
                  pl.BlockSpec((D_MODEL, FF_TC), lambda bi, i, c: (0, c)),
                  pl.BlockSpec((D_MODEL, FF_TC), lambda bi, i, c: (0, FF_NC + c)),
                  pl.BlockSpec((FFN_K, FF_TC), lambda bi, i, c: (0, c)),
                  pl.BlockSpec((FFN_K, FF_TC), lambda bi, i, c: (0, FF_NC + c)),
                  pl.BlockSpec((1, FF_TC), lambda bi, i, c: (0, c)),
                  pl.BlockSpec((1, FF_TC), lambda bi, i, c: (0, FF_NC + c)),
                  pl.BlockSpec((FF_TC, D_MODEL), lambda bi, i, c: (c, 0)),
                  vec, vec],
        out_specs=(pl.BlockSpec((None, tm, D_MODEL), lambda bi, i, c: (bi, i, 0)),
                   pl.BlockSpec((None, nk, D_FF), lambda bi, i, c: (bi, 0, 0)),
                   pl.BlockSpec((None, nk, D_FF), lambda bi, i, c: (bi, 0, 0))),
        scratch_shapes=[pltpu.VMEM((tm, D_MODEL), BF16),
                        pltpu.VMEM((tm, D_MODEL), F32),
                        pltpu.VMEM((FF_NC, 8, FF_TC), F32),
                        pltpu.VMEM((FF_NC, 8, FF_TC), F32),
                        pltpu.VMEM((tm + 8, FF_TC), F32)],
        compiler_params=_cparams(("parallel", "arbitrary", "arbitrary")),
        name="conv_ffn_ln2",
    )(x3, st, st, w_up, w_up, cw, cw, cb, cb, w_dn, g, beta)


def _rope_tables(past, t):
    half = A_ROPE // 2
    freqs = ROPE_BASE ** (-jnp.arange(half, dtype=F32) / half)
    pos = past + jnp.arange(t, dtype=jnp.int32)
    ang = pos.astype(F32)[:, None] * freqs[None, :]
    cos, sin = jnp.cos(ang), jnp.sin(ang)
    c2 = jnp.concatenate([cos, cos], axis=1)
    s2 = jnp.concatenate([sin, sin], axis=1)
    one = jnp.ones((t, A_NOPE), F32)
    z = lambda n: jnp.zeros((t, n), F32)
    cosq = jnp.concatenate([one, c2, z(LANES - A_NOPE - A_ROPE)], axis=1)
    sinq = jnp.concatenate([z(A_NOPE), s2, z(LANES - A_NOPE - A_ROPE)], axis=1)
    cosk = jnp.concatenate([c2, z(LANES - A_ROPE)], axis=1)
    sink = jnp.concatenate([s2, z(LANES - A_ROPE)], axis=1)
    return cosq, sinq, cosk, sink


_BUCKET_EDGES = (12, 16, 23, 32, 46, 64, 91)


def _rel_bucket(rel):
    nb = REL_BUCKETS // 2
    max_exact = nb // 2
    n = np.abs(rel)
    large = max_exact + sum((n >= e).astype(np.int64) for e in _BUCKET_EDGES)
    return np.where(rel > 0, nb, 0) + np.where(n < max_exact, n, large)


def _bias_tiles(rel_bias, tq):
    r = np.arange(tq)[:, None]
    c = np.arange(LANES)[None, :]
    buckets = np.stack([_rel_bucket(c - r - LANES * d) for d in range(3)])
    tiles = rel_bias[jnp.asarray(buckets, dtype=jnp.int32)]
    return jnp.transpose(tiles, (0, 3, 1, 2))


def _pack_w_in(w):
    z = lambda n: jnp.zeros((D_MODEL, n), F32)
    kr = w[:, O_KROPE:O_KROPE + A_ROPE]
    half = A_ROPE // 2
    kr_rot = jnp.concatenate([-kr[:, half:], kr[:, :half]], axis=1)
    parts = [
        w[:, O_GATES:O_GATES + N_BRANCH * D_MODEL],
        w[:, O_CQ:O_CQ + A_Q_LORA], z(256 - A_Q_LORA),
        w[:, O_CKV:O_CKV + A_KV_LORA], kr, kr_rot, z(256 - A_KV_LORA - 2 * A_ROPE),
        w[:, O_BQ:O_BQ + 256], w[:, O_BK:O_BK + 256], w[:, O_BV:O_BV + 256],
        w[:, O_QIDX:O_QIDX + 128], w[:, O_KIDX:O_KIDX + IDX_DIM], w[:, O_WIDX:O_WIDX + IDX_HEADS],
        z(256 - 128 - IDX_DIM - IDX_HEADS),
        w[:, O_CONV:O_CONV + 2 * CONV_WIDTH],
        w[:, O_POOL:O_POOL + POOL_WIDTH],
    ]
    return jnp.concatenate(parts, axis=1).astype(BF16)


def _pack_w_qup(w):
    zrow = lambda m: jnp.zeros((256 - A_Q_LORA, m.shape[1]), F32)
    half = A_ROPE // 2
    dq = A_NOPE + A_ROPE
    cols, rcols = [], []
    for h in range(A_HEADS):
        nope = w[:, h * dq:h * dq + A_NOPE]
        rp = w[:, h * dq + A_NOPE:(h + 1) * dq]
        pad = jnp.zeros((A_Q_LORA, LANES - dq), F32)
        cols += [nope, rp, pad]
        rcols += [jnp.zeros_like(nope), -rp[:, half:], rp[:, :half], pad]
    wq = jnp.concatenate(cols, axis=1)
    wqr = jnp.concatenate(rcols, axis=1)
    wq = jnp.concatenate([wq, zrow(wq)], axis=0).astype(BF16)
    wqr = jnp.concatenate([wqr, zrow(wqr)], axis=0).astype(BF16)
    return wq, wqr


def _pack_w_kvup(w):
    dk = A_NOPE + A_V
    kc, vc = [], []
    for h in range(A_HEADS):
        kc += [w[:, h * dk:h * dk + A_NOPE], jnp.zeros((A_KV_LORA, LANES - A_NOPE), F32)]
        vc += [w[:, h * dk + A_NOPE:(h + 1) * dk]]
    return jnp.concatenate(kc, axis=1).astype(BF16), jnp.concatenate(vc, axis=1).astype(BF16)


def _rope_placement():
    e = np.zeros((LANES, A_HEADS * LANES), np.float32)
    for h in range(A_HEADS):
        for r in range(A_ROPE):
            e[r, h * LANES + A_NOPE + r] = 1.0
    return jnp.asarray(e, dtype=BF16)


def _pool_blockdiag(pw):
    out = jnp.zeros((POOL_WIDTH, POOL_WIDTH), F32)
    for g in range(len(POOL_WINDOWS)):
        out = out.at[g * POOL_GROUP:(g + 1) * POOL_GROUP, g * POOL_GROUP:(g + 1) * POOL_GROUP].set(pw[g])
    return out.astype(BF16)


class _Tiles:
    def __init__(self, t):
        prompt = t > 128
        self.rows = 1024 if prompt else 128
        self.in_tn = 1280
        self.prep = 512 if prompt else t
        self.kv = 512
        self.mla_q = 256 if prompt else t
        self.mla_k = 512 if prompt else 256
        self.dsa_q = 128 if prompt else t
        self.dsa_kc = 512 if prompt else 256
        self.pool = 512 if prompt else t
        self.conv = 256 if prompt else t
        self.conv_sub = 64 if prompt else t
        self.merge = 512 if prompt else t
        self.ffn = 1024 if prompt else t
        self.s_align = 512 if prompt else 256


def _trunk_layer(x3, caches, lw, consts, past):
    (w_in, gq, gkv, wq, wqr, wkn, wv, pool_bd, pool_scale, conv_w, conv_b, conv_g, conv_beta,
     w_branch, w_out, ln1_g, ln1_b, w_up, ffn_cw, ffn_cb, w_down, ln2_g, ln2_b) = lw
    tabs, bias, e_place = consts
    b, t, _ = x3.shape
    tl = _Tiles(t)
    s_valid = past + t
    s_pad = -(-s_valid // tl.s_align) * tl.s_align
    topk = min(TOPK_MAX, s_valid // 4)

    proj = _inproj(x3.reshape(b * t, D_MODEL), w_in, tl.rows, tl.in_tn)
    proj3 = proj.reshape(b, t, PROJ_W)
    qa, ckv_n, kr128 = _aprep(proj3, tabs, gq, gkv, wq, wqr, tl.prep)

    col = lambda blk, off, n: proj3[:, :, blk * 256 + off:blk * 256 + off + n]
    bk_new = col(CB_BK, 0, 256)
    bv_new = col(CB_BV, 0, 256)
    kidx_new = col(CB_IDX, 128, IDX_DIM)
    krope_new = kr128[:, :, :A_ROPE]

    def keys(cache, new, width):
        parts = [new] if cache is None else [cache.reshape(b, past, -1), new]
        if s_pad > s_valid:
            parts.append(jnp.zeros((b, s_pad - s_valid, width), new.dtype))
        return parts[0] if len(parts) == 1 else jnp.concatenate(parts, axis=1)

    c_ckv, c_krope, c_bk, c_bv, c_kidx, st_pool, st_conv, st_ffn = caches
    ckv_all = keys(c_ckv, ckv_n, A_KV_LORA)
    if c_krope is None:
        kr_all = keys(None, kr128, LANES)
    else:
        c_kr128 = jnp.pad(c_krope, ((0, 0), (0, 0), (0, LANES - A_ROPE)))
        kr_all = keys(c_kr128, kr128, LANES)
    head_major = lambda a: jnp.transpose(a.astype(BF16).reshape(b, s_pad, B_HEADS, B_DIM),
                                         (0, 2, 1, 3))
    bk_all = head_major(keys(c_bk, bk_new, 256))
    bv_all = head_major(keys(c_bv, bv_new, 256))
    ki_all = jnp.tile(keys(c_kidx, kidx_new, IDX_DIM).astype(BF16), (1, 1, IDX_HEADS))

    kx, vx = _kvup(ckv_all.reshape(b * s_pad, A_KV_LORA), kr_all.reshape(b * s_pad, LANES),
                   wkn, wv, e_place, tl.kv)
    out_a = _mla(qa, kx.reshape(b, s_pad, 512), vx.reshape(A_HEADS, b, s_pad, A_V),
                 tl.mla_q, tl.mla_k, past, s_valid)
    out_b = _dsa(proj3, bk_all, bv_all, ki_all, bias, tl.dsa_q, tl.dsa_kc, past, s_valid, topk)
    out_c, new_pool = _pool(proj3, st_pool, pool_bd, pool_scale, tl.pool, past)
    out_d, new_conv = _conv(proj3, st_conv, conv_w, conv_b, conv_g, conv_beta, tl.conv, tl.conv_sub)
    x1 = _merge(out_a, out_b, out_c, out_d, proj3, x3, w_branch, w_out, ln1_g, ln1_b, tl.merge)
    x2, nst_v, nst_g = _ffn(x1, st_ffn, w_up, ffn_cw, ffn_cb, w_down, ln2_g, ln2_b, tl.ffn)
    new_ffn = jnp.concatenate([nst_v, nst_g], axis=-1)
    states = (ckv_n, krope_new, bk_new.reshape(b, t, B_HEADS, B_DIM),
              bv_new.reshape(b, t, B_HEADS, B_DIM), kidx_new, new_pool, new_conv, new_ffn)
    return x2, states


def kernel(x_prompt, x_sample, cache_a_ckv, cache_a_krope, cache_b_k, cache_b_v, cache_b_kidx,
           state_pool, state_conv, state_ffn, rel_bias, ln_in_g, ln_in_b, w_in, a_q_norm, a_kv_norm,
           a_w_qup, a_w_kvup, pool_w, pool_scale, conv_w, conv_b, conv_ln_g, conv_ln_b, w_branch,
           w_out, ln1_g, ln1_b, w_up, ffn_conv_w, ffn_conv_b, w_down, ln2_g, ln2_b):
    bp, tp, _ = x_prompt.shape
    bs, ts, _ = x_sample.shape
    past = cache_a_ckv.shape[2]
    depth = w_in.shape[0]

    xp = _layer_norm(x_prompt.reshape(bp * tp, D_MODEL), ln_in_g, ln_in_b,
                     _Tiles(tp).rows).reshape(bp, tp, D_MODEL)
    xs = _layer_norm(x_sample.reshape(bs * ts, D_MODEL), ln_in_g, ln_in_b,
                     _Tiles(ts).rows).reshape(bs, ts, D_MODEL)

    e_place = _rope_placement()
    consts_p = (_rope_tables(0, tp), _bias_tiles(rel_bias, _Tiles(tp).dsa_q), e_place)
    consts_s = (_rope_tables(past, ts), _bias_tiles(rel_bias, _Tiles(ts).dsa_q), e_place)
    zeros_p = (None, None, None, None, None,
               jnp.zeros((bp, POOL_STATE, POOL_WIDTH), F32),
               jnp.zeros((bp, CONV_K - 1, CONV_WIDTH), F32),
               jnp.zeros((bp, FFN_K - 1, 2 * D_FF), F32))

    row = lambda v: v.reshape(1, -1)
    p_states, s_states = [], []
    for l in range(depth):
        wq, wqr = _pack_w_qup(a_w_qup[l])
        wkn, wv = _pack_w_kvup(a_w_kvup[l])
        gq = jnp.concatenate([a_q_norm[l], jnp.zeros((256 - A_Q_LORA,), F32)]).reshape(1, 256)
        lw = (_pack_w_in(w_in[l]), gq, row(a_kv_norm[l]), wq, wqr, wkn, wv,
              _pool_blockdiag(pool_w[l]), row(pool_scale[l]), conv_w[l], row(conv_b[l]),
              row(conv_ln_g[l]), row(conv_ln_b[l]), w_branch[l].astype(BF16), w_out[l].astype(BF16),
              row(ln1_g[l]), row(ln1_b[l]), w_up[l].astype(BF16), ffn_conv_w[l], row(ffn_conv_b[l]),
              w_down[l].astype(BF16), row(ln2_g[l]), row(ln2_b[l]))
        xp, st_p = _trunk_layer(xp, zeros_p, lw, consts_p, 0)
        caches_s = (cache_a_ckv[l], cache_a_krope[l], cache_b_k[l], cache_b_v[l], cache_b_kidx[l],
                    state_pool[l], state_conv[l], state_ffn[l])
        xs, st_s = _trunk_layer(xs, caches_s, lw, consts_s, past)
        p_states.append(st_p)
        s_states.append(st_s)

    p_out = [jnp.stack(a) for a in zip(*p_states)]
    s_out = [jnp.stack(a) for a in zip(*s_states)]
    return (xp, xs, *p_out, *s_out)
```

```python
import functools

import numpy as np
import jax
import jax.numpy as jnp
from jax import lax
from jax.experimental import pallas as pl
from jax.experimental.pallas import tpu as pltpu

F32 = jnp.float32
BF16 = jnp.bfloat16

D_MODEL = 1024
DEPTH = 4
CHUNK = 64
CHUNK_SHIFT = 6
N_BRANCH = 4
A_HEADS = 4
A_NOPE = 64
A_ROPE = 32
A_V = 64
A_Q_LORA = 192
A_KV_LORA = 128
A_SCALE = (A_NOPE + A_ROPE) ** -0.5
ROPE_BASE = 10000.0
B_HEADS = 4
B_DIM = 64
B_SCALE = B_DIM ** -0.5
IDX_HEADS = 4
IDX_DIM = 32
IDX_SCALE = (IDX_HEADS ** -0.5) * (IDX_DIM ** -0.5)
TOPK_MAX = 256
REL_BUCKETS = 32
POOL_WINDOWS = (2, 4, 8, 16)
POOL_WIDTH = 256
POOL_GROUP = 64
POOL_STATE = 15
CONV_WIDTH = 256
CONV_K = 31
D_FF = 2816
FFN_K = 3
ALPHA = (2 * DEPTH) ** 0.25
LN_EPS = 1e-5

IN_SPLITS = (A_Q_LORA, A_KV_LORA, A_ROPE, 256, 256, 256, 128, IDX_DIM, IDX_HEADS,
             POOL_WIDTH, 2 * CONV_WIDTH, N_BRANCH * D_MODEL)
_OFF = [0] + [int(v) for v in np.cumsum(IN_SPLITS)]
(O_CQ, O_CKV, O_KROPE, O_BQ, O_BK, O_BV, O_QIDX, O_KIDX, O_WIDX, O_POOL, O_CONV, O_GATES,
 D_IN) = _OFF

PROJ_W = 6400
CB_GATES = 0
CB_CQ = 16
CB_CK = 17
CB_BQ = 18
CB_BK = 19
CB_BV = 20
CB_IDX = 21
CB_CONV = 11
CB_POOL = 24

VMEM_LIMIT_BYTES = 48 * 1024 * 1024
LANES = 128

NEG = -1e30
M_INIT = -5e29
SEARCH_MAX_IT = 320
SNAP_EVERY = 8

_NT = (((1,), (1,)), ((), ()))


def _cparams(sem):
    return pltpu.CompilerParams(dimension_semantics=sem, vmem_limit_bytes=VMEM_LIMIT_BYTES)


def _ln_rows(x, g, b):
    mu = jnp.mean(x, axis=-1, keepdims=True)
    xc = x - mu
    var = jnp.mean(xc * xc, axis=-1, keepdims=True)
    return xc * lax.rsqrt(var + LN_EPS) * g + b


def _sigmoid(x):
    return 1.0 / (1.0 + jnp.exp(-x))


def _ln_kernel(x_ref, g_ref, b_ref, o_ref):
    o_ref[...] = _ln_rows(x_ref[...], g_ref[...], b_ref[...])


def _layer_norm(x2d, g, b, tm):
    n, d = x2d.shape
    return pl.pallas_call(
        _ln_kernel,
        out_shape=jax.ShapeDtypeStruct((n, d), F32),
        grid=(n // tm,),
        in_specs=[pl.BlockSpec((tm, d), lambda i: (i, 0)),
                  pl.BlockSpec((1, d), lambda i: (0, 0)),
                  pl.BlockSpec((1, d), lambda i: (0, 0))],
        out_specs=pl.BlockSpec((tm, d), lambda i: (i, 0)),
        compiler_params=_cparams(("parallel",)),
        name="input_ln",
    )(x2d, g.reshape(1, d), b.reshape(1, d))


def _inproj_kernel(x_ref, w_ref, o_ref, xb_ref):
    @pl.when(pl.program_id(1) == 0)
    def _():
        xb_ref[...] = x_ref[...].astype(BF16)

    o_ref[...] = jnp.dot(xb_ref[...], w_ref[...], preferred_element_type=F32)


def _inproj(x2d, w, tm, tn):
    n, k = x2d.shape
    nw = w.shape[1]
    return pl.pallas_call(
        _inproj_kernel,
        out_shape=jax.ShapeDtypeStruct((n, nw), F32),
        grid=(n // tm, nw // tn),
        in_specs=[pl.BlockSpec((tm, k), lambda i, j: (i, 0)),
                  pl.BlockSpec((k, tn), lambda i, j: (0, j))],
        out_specs=pl.BlockSpec((tm, tn), lambda i, j: (i, j)),
        scratch_shapes=[pltpu.VMEM((tm, k), BF16)],
        compiler_params=_cparams(("parallel", "arbitrary")),
        name="in_proj",
    )(x2d, w)


def _aprep_kernel(cq_ref, ck_ref, cosq_ref, sinq_ref, cosk_ref, sink_ref, gq_ref, gkv_ref,
                  wq_ref, wqr_ref, qa_ref, ckv_ref, kr_ref):
    cq = cq_ref[...]
    ms = jnp.sum(cq * cq, axis=-1, keepdims=True) * (1.0 / A_Q_LORA)
    cqb = ((cq * lax.rsqrt(ms + LN_EPS)) * gq_ref[...]).astype(BF16)
    q = jnp.dot(cqb, wq_ref[...], preferred_element_type=F32)
    qr = jnp.dot(cqb, wqr_ref[...], preferred_element_type=F32)
    cos = jnp.concatenate([cosq_ref[...]] * A_HEADS, axis=1)
    sin = jnp.concatenate([sinq_ref[...]] * A_HEADS, axis=1)
    qa_ref[...] = ((q * cos + qr * sin) * A_SCALE).astype(BF16)

    ck = ck_ref[...]
    ckv = ck[:, :A_KV_LORA]
    ms2 = jnp.mean(ckv * ckv, axis=-1, keepdims=True)
    ckv_ref[...] = (ckv * lax.rsqrt(ms2 + LN_EPS)) * gkv_ref[...]
    kr = ck[:, A_KV_LORA:]
    krr = pltpu.roll(kr, LANES - A_ROPE, 1)
    kr_ref[...] = kr * cosk_ref[...] + krr * sink_ref[...]


def _aprep(proj3, tabs, gq, gkv, wq, wqr, tm):
    b, t, _ = proj3.shape
    cosq, sinq, cosk, sink = tabs
    tab_spec = pl.BlockSpec((tm, LANES), lambda bi, i: (i, 0))
    full = lambda shape: pl.BlockSpec(shape, lambda bi, i: (0,) * len(shape))
    return pl.pallas_call(
        _aprep_kernel,
        out_shape=(jax.ShapeDtypeStruct((b, t, 512), BF16),
                   jax.ShapeDtypeStruct((b, t, A_KV_LORA), F32),
                   jax.ShapeDtypeStruct((b, t, LANES), F32)),
        grid=(b, t // tm),
        in_specs=[pl.BlockSpec((None, tm, 256), lambda bi, i: (bi, i, CB_CQ)),
                  pl.BlockSpec((None, tm, 256), lambda bi, i: (bi, i, CB_CK)),
                  tab_spec, tab_spec, tab_spec, tab_spec,
                  full((1, 256)), full((1, A_KV_LORA)), full((256, 512)), full((256, 512))],
        out_specs=(pl.BlockSpec((None, tm, 512), lambda bi, i: (bi, i, 0)),
                   pl.BlockSpec((None, tm, A_KV_LORA), lambda bi, i: (bi, i, 0)),
                   pl.BlockSpec((None, tm, LANES), lambda bi, i: (bi, i, 0))),
        compiler_params=_cparams(("parallel", "parallel")),
        name="mla_prep",
    )(proj3, proj3, cosq, sinq, cosk, sink, gq, gkv, wq, wqr)


def _kvup_kernel(ckv_ref, kr_ref, wkn_ref, wv_ref, e_ref, k_ref, v_ref):
    cb = ckv_ref[...].astype(BF16)
    k = jnp.dot(cb, wkn_ref[...], preferred_element_type=F32)
    k = k + jnp.dot(kr_ref[...].astype(BF16), e_ref[...], preferred_element_type=F32)
    k_ref[...] = k.astype(BF16)
    v = jnp.dot(cb, wv_ref[...], preferred_element_type=F32).astype(BF16)
    for h in range(A_HEADS):
        v_ref[h] = v[:, h * A_V:(h + 1) * A_V]


def _kvup(ckv2d, kr2d, wkn, wv, e, tm):
    m = ckv2d.shape[0]
    full = lambda shape: pl.BlockSpec(shape, lambda i: (0,) * len(shape))
    return pl.pallas_call(
        _kvup_kernel,
        out_shape=(jax.ShapeDtypeStruct((m, 512), BF16),
                   jax.ShapeDtypeStruct((A_HEADS, m, A_V), BF16)),
        grid=(m // tm,),
        in_specs=[pl.BlockSpec((tm, A_KV_LORA), lambda i: (i, 0)),
                  pl.BlockSpec((tm, LANES), lambda i: (i, 0)),
                  full((A_KV_LORA, 512)), full((A_KV_LORA, 256)), full((LANES, 512))],
        out_specs=(pl.BlockSpec((tm, 512), lambda i: (i, 0)),
                   pl.BlockSpec((A_HEADS, tm, A_V), lambda i: (0, i, 0))),
        compiler_params=_cparams(("parallel",)),
        name="kv_up",
    )(ckv2d, kr2d, wkn, wv, e)


def _online(carry, s, v):
    m, l, acc = carry
    m_new = jnp.maximum(m, jnp.max(s, axis=1, keepdims=True))
    alpha = jnp.exp(m - m_new)
    p = jnp.exp(s - m_new)
    l = alpha * l + jnp.sum(p, axis=1, keepdims=True)
    acc = alpha * acc + jnp.dot(p.astype(BF16), v, preferred_element_type=F32)
    return m_new, l, acc


def _softmax_init(tq, dv):
    return (jnp.full((tq, 1), M_INIT, F32), jnp.zeros((tq, 1), F32), jnp.zeros((tq, dv), F32))


def _visible_limit(qpos, s_valid):
    return jnp.minimum((jnp.right_shift(qpos, CHUNK_SHIFT) + 1) * CHUNK, s_valid)


def _mla_kernel(q_ref, k_ref, v_ref, o_ref, *, tq, tk, past, s_valid):
    i = pl.program_id(1)
    q0 = past + i * tq
    qpos = q0 + lax.broadcasted_iota(jnp.int32, (tq, 1), 0)
    q_lim = _visible_limit(qpos, s_valid)
    lim_first = _visible_limit(q0, s_valid)
    lim_last = _visible_limit(q0 + tq - 1, s_valid)
    n_full = lim_first // tk
    n_tot = (lim_last + tk - 1) // tk
    q = q_ref[...]
    qhs = [q[:, h * LANES:(h + 1) * LANES] for h in range(A_HEADS)]

    def body(j, carries, masked):
        ks = pl.ds(pl.multiple_of(j * tk, tk), tk)
        if masked:
            kpos = j * tk + lax.broadcasted_iota(jnp.int32, (1, tk), 1)
            vis = kpos < q_lim
        out = []
        for h in range(A_HEADS):
            s = lax.dot_general(qhs[h], k_ref[ks, h * LANES:(h + 1) * LANES], _NT,
                                preferred_element_type=F32)
            if masked:
                s = jnp.where(vis, s, NEG)
            out.append(_online(carries[h], s, v_ref[h, ks, :]))
        return tuple(out)

    init = tuple(_softmax_init(tq, A_V) for _ in range(A_HEADS))
    carries = lax.fori_loop(0, n_full, functools.partial(body, masked=False), init)
    carries = lax.fori_loop(n_full, n_tot, functools.partial(body, masked=True), carries)
    for h in range(A_HEADS):
        _, l, acc = carries[h]
        o_ref[:, h * A_V:(h + 1) * A_V] = acc / l


def _mla(qa, kx, v, tq, tk, past, s_valid):
    b, t, _ = qa.shape
    s_pad = kx.shape[1]
    kern = functools.partial(_mla_kernel, tq=tq, tk=tk, past=past, s_valid=s_valid)
    return pl.pallas_call(
        kern,
        out_shape=jax.ShapeDtypeStruct((b, t, 256), F32),
        grid=(b, t // tq),
        in_specs=[pl.BlockSpec((None, tq, 512), lambda bi, i: (bi, i, 0)),
                  pl.BlockSpec((None, s_pad, 512), lambda bi, i: (bi, 0, 0)),
                  pl.BlockSpec((A_HEADS, None, s_pad, A_V), lambda bi, i: (0, bi, 0, 0))],
        out_specs=pl.BlockSpec((None, tq, 256), lambda bi, i: (bi, i, 0)),
        compiler_params=_cparams(("parallel", "arbitrary")),
        name="mla_attn",
    )(qa, kx, v)


KEY_NONE = -2 ** 31
KEY_MAX = 2 ** 31 - 1


def _order_key(x):
    b = lax.bitcast_convert_type(x, jnp.int32)
    return b ^ (jnp.right_shift(b, 31) & KEY_MAX)


def _order_key_inv(k):
    return lax.bitcast_convert_type(k ^ (jnp.right_shift(k, 31) & KEY_MAX), F32)


def _dsa_kernel(bq_ref, idx_ref, bk_ref, bv_ref, ki_ref, bias_ref, o_ref, sc_ref, sct_ref, msk_ref,
                *, tq, tqs, tkc, past, s_valid, topk):
    i = pl.program_id(1)
    q0 = past + i * tq
    qpos = q0 + lax.broadcasted_iota(jnp.int32, (tq, 1), 0)
    q_lim = _visible_limit(qpos, s_valid)
    lim_last = _visible_limit(q0 + tq - 1, s_valid)
    nch = (lim_last + tkc - 1) // tkc
    ngrp = tkc // LANES
    kf = float(topk)

    def chunk(c):
        return pl.ds(pl.multiple_of(c * tkc, tkc), tkc)

    def key_pos(c):
        return c * tkc + lax.broadcasted_iota(jnp.int32, (1, tkc), 1)

    def key_pos_col(c):
        return (c * tkc + lax.broadcasted_iota(jnp.int32, (tkc, 1), 0)).astype(F32)

    idx = idx_ref[...]
    lane = lax.broadcasted_iota(jnp.int32, (1, LANES), 1)
    qi = idx[:, :LANES]
    qhs = [jnp.where((lane >= h * IDX_DIM) & (lane < (h + 1) * IDX_DIM), qi, 0.0).astype(BF16)
           for h in range(IDX_HEADS)]
    wi = idx[:, LANES + IDX_DIM:LANES + IDX_DIM + IDX_HEADS] * IDX_SCALE
    whs = [wi[:, h:h + 1] for h in range(IDX_HEADS)]

    def score_chunk(c, carry):
        ki = ki_ref[chunk(c), :]
        tot = None
        for h in range(IDX_HEADS):
            d = lax.dot_general(qhs[h], ki, _NT, preferred_element_type=F32)
            term = whs[h] * jnp.maximum(d, 0.0)
            tot = term if tot is None else tot + term
        sc = jnp.where(key_pos(c) < q_lim, _order_key(tot), KEY_NONE)
        sc_ref[:, chunk(c)] = sc
        if tqs > tq:
            sc = jnp.concatenate([sc, jnp.full((tqs - tq, tkc), KEY_NONE, jnp.int32)], axis=0)
        for g in range(ngrp):
            rows = pl.ds(pl.multiple_of(c * tkc + g * LANES, LANES), LANES)
            for rg in range(tqs // LANES):
                sct_ref[rows, rg * LANES:(rg + 1) * LANES] = (
                    sc[rg * LANES:(rg + 1) * LANES, g * LANES:(g + 1) * LANES].T)
        return carry

    lax.fori_loop(0, nch, score_chunk, 0)

    def fold8(x, op):
        pair = {jnp.sum: jnp.add, jnp.min: jnp.minimum, jnp.max: jnp.maximum}[op]
        parts = [x[8 * r:8 * (r + 1)] for r in range(tkc // 8)]
        while len(parts) > 1:
            odd = parts[len(parts) - 1:] if len(parts) % 2 else []
            parts = [pair(parts[r], parts[r + 1]) for r in range(0, len(parts) - 1, 2)] + odd
        return parts[0]

    def count_where(pred):
        def body(c, acc):
            m = jnp.where(pred(sct_ref[chunk(c), :], c), 1.0, 0.0)
            return acc + fold8(m, jnp.sum)
        acc = lax.fori_loop(0, nch, body, jnp.zeros((8, tqs), F32))
        return jnp.sum(acc, axis=0, keepdims=True)

    def count_gt(t):
        return count_where(lambda x, c: x > t)

    def search():
        lane_q = lax.broadcasted_iota(jnp.int32, (1, tqs), 1)
        nvis = jnp.where(lane_q < tq, _visible_limit(q0 + lane_q, s_valid), 0).astype(F32)

        def mm_body(c, carry):
            mn, mx = carry
            x = sct_ref[chunk(c), :]
            xm = jnp.where(x == KEY_NONE, KEY_MAX, x)
            return (jnp.minimum(mn, fold8(xm, jnp.min)), jnp.maximum(mx, fold8(x, jnp.max)))
        mn, mx = lax.fori_loop(0, nch, mm_body, (jnp.full((8, tqs), KEY_MAX, jnp.int32),
                                                 jnp.full((8, tqs), KEY_NONE, jnp.int32)))
        mn = jnp.min(mn, axis=0, keepdims=True)
        mx = jnp.max(mx, axis=0, keepdims=True)
        done0 = jnp.where(nvis <= kf, 1.0, 0.0)
        mn = jnp.where(nvis > 0.5, mn, 0)
        mx = jnp.where(nvis > 0.5, mx, 0)
        lo0 = mn - 1
        zeros = jnp.zeros((1, tqs), F32)
        izeros = jnp.zeros((1, tqs), jnp.int32)

        def snap(lo, hi, done, tie, vt):
            def body(c, carry):
                a, b = carry
                x = sct_ref[chunk(c), :]
                xa = jnp.where(x > lo, x, KEY_MAX)
                xb = jnp.where(x <= hi, x, KEY_NONE)
                return (jnp.minimum(a, fold8(xa, jnp.min)), jnp.maximum(b, fold8(xb, jnp.max)))
            a, b = lax.fori_loop(0, nch, body, (jnp.full((8, tqs), KEY_MAX, jnp.int32),
                                                jnp.full((8, tqs), KEY_NONE, jnp.int32)))
            a = jnp.min(a, axis=0, keepdims=True)
            b = jnp.max(b, axis=0, keepdims=True)
            nd = done < 0.5
            new_tie = nd & (a >= b)
            hi = jnp.where(nd & (a < b), b, hi)
            vt = jnp.where(new_tie, b, vt)
            tie = jnp.where(new_tie, 1.0, tie)
            done = jnp.where(new_tie, 1.0, done)
            return lo, hi, done, tie, vt

        def cond(s):
            return (s[1] > 0) & (s[0] < SEARCH_MAX_IT)

        def one_pass(it, s):
            lo, hi, clo, chi, t, done, tie, vt = s
            stuck = hi <= lo + 1
            lo_f, hi_f = _order_key_inv(lo), _order_key_inv(hi)
            itp = lo_f + (hi_f - lo_f) * ((clo - kf) / (clo - chi))
            cand = _order_key(jnp.where(it % 2 == 1, itp, 0.5 * lo_f + 0.5 * hi_f))
            imid = (lo & hi) + jnp.right_shift(lo ^ hi, 1)
            mid = jnp.where((cand > lo) & (cand < hi), cand, imid)
            c = count_gt(mid)
            nd = done < 0.5
            live = nd & jnp.logical_not(stuck)
            hit = live & (c == kf)
            stk = nd & stuck
            up = live & (c > kf)
            dn = live & (c < kf)
            t = jnp.where(hit, mid, t)
            vt = jnp.where(stk, hi, vt)
            tie = jnp.where(stk, 1.0, tie)
            lo = jnp.where(up, mid, lo)
            clo = jnp.where(up, c, clo)
            hi = jnp.where(dn, mid, hi)
            chi = jnp.where(dn, c, chi)
            done = jnp.where(hit | stk, 1.0, done)
            return lo, hi, clo, chi, t, done, tie, vt

        def body(s):
            it = s[0]
            lo, hi, clo, chi, t, done, tie, vt = one_pass(it + 1, one_pass(it, s[2:]))
            lo, hi, done, tie, vt = lax.cond(
                it % SNAP_EVERY == SNAP_EVERY - 2, snap,
                lambda *a: a, lo, hi, done, tie, vt)
            active = (jnp.max(1.0 - done) > 0.5).astype(jnp.int32)
            return it + 2, active, lo, hi, clo, chi, t, done, tie, vt

        active0 = (jnp.max(1.0 - done0) > 0.5).astype(jnp.int32)
        s = lax.while_loop(cond, body, (jnp.int32(0), active0, lo0, mx, nvis, zeros, lo0, done0,
                                        zeros, izeros))
        t, tie, vt = s[6], s[8], s[9]

        def tie_cut():
            t2 = jnp.where(tie > 0.5, vt, t)
            need = kf - count_gt(t2)

            def bs_body(_, carry):
                lo_c, hi_c = carry
                mid_c = jnp.floor((lo_c + hi_c) * 0.5)
                cnt = count_where(lambda x, c: (x == vt) & (key_pos_col(c) < mid_c))
                ok = cnt >= need
                return jnp.where(ok, lo_c, mid_c), jnp.where(ok, mid_c, hi_c)

            nbits = int(s_valid).bit_length()
            _, cut = lax.fori_loop(0, nbits, bs_body,
                                   (zeros, jnp.full((1, tqs), float(1 << nbits), F32)))
            return t2, jnp.where(tie > 0.5, cut, 0.0)

        any_tie = jnp.max(tie) > 0.5
        t, cut = lax.cond(any_tie, tie_cut, lambda: (t, zeros))
        return t, tie, vt, cut

    def no_search():
        z = jnp.zeros((1, tqs), F32)
        return (jnp.full((1, tqs), KEY_NONE, jnp.int32), z, jnp.zeros((1, tqs), jnp.int32), z)

    t, tie, vt, cut = lax.cond(lim_last > topk, search, no_search)

    def to_col(v):
        groups = [jnp.broadcast_to(v[:, rg * LANES:(rg + 1) * LANES], (LANES, LANES)).T
                  for rg in range(tqs // LANES)]
        full = groups[0] if len(groups) == 1 else jnp.concatenate(groups, axis=0)
        return full[:tq]

    t, vt, cut = to_col(t), to_col(vt), to_col(cut)
    tied = to_col(tie) > 0.5
    lane_f = lax.broadcasted_iota(jnp.int32, (1, LANES), 1).astype(F32)

    def to_mask(c, carry):
        for g in range(ngrp):
            cols = pl.ds(pl.multiple_of(c * tkc + g * LANES, LANES), LANES)
            x = sc_ref[:, cols]
            kpos = (c * tkc + g * LANES).astype(F32) + lane_f
            sel = (x > t) | (tied & (x == vt) & (kpos < cut))
            msk_ref[:, cols] = jnp.where(sel, 0.0, NEG)
        return carry

    lax.fori_loop(0, nch, to_mask, 0)

    bq = (bq_ref[...] * B_SCALE).astype(BF16)
    qbs = [bq[:, h * B_DIM:(h + 1) * B_DIM] for h in range(B_HEADS)]
    d0 = q0 // LANES

    def att_body(c, carries):
        ks = chunk(c)
        msk = msk_ref[:, ks]
        rows = min(tq, LANES)
        out = []
        for h in range(B_HEADS):
            s = lax.dot_general(qbs[h], bk_ref[h, ks, :], _NT, preferred_element_type=F32)
            bias = jnp.concatenate(
                [jnp.concatenate([bias_ref[jnp.clip(d0 + rg - (c * ngrp + g), 0, 2), h, :rows, :]
                                  for g in range(ngrp)], axis=1)
                 for rg in range(-(-tq // LANES))], axis=0)
            out.append(_online(carries[h], s + (msk + bias), bv_ref[h, ks, :]))
        return tuple(out)

    init = tuple(_softmax_init(tq, B_DIM) for _ in range(B_HEADS))
    carries = lax.fori_loop(0, nch, att_body, init)
    for h in range(B_HEADS):
        _, l, acc = carries[h]
        o_ref[:, h * B_DIM:(h + 1) * B_DIM] = acc / l


def _dsa(proj3, bk, bv, ki, bias, tq, tkc, past, s_valid, topk):
    b, t, _ = proj3.shape
    s_pad = bk.shape[2]
    tqs = -(-tq // LANES) * LANES
    assert past % LANES == 0 and (tq % LANES == 0 or t == tq) and s_pad % tkc == 0
    kern = functools.partial(_dsa_kernel, tq=tq, tqs=tqs, tkc=tkc, past=past, s_valid=s_valid,
                             topk=topk)
    return pl.pallas_call(
        kern,
        out_shape=jax.ShapeDtypeStruct((b, t, 256), F32),
        grid=(b, t // tq),
        in_specs=[pl.BlockSpec((None, tq, 256), lambda bi, i: (bi, i, CB_BQ)),
                  pl.BlockSpec((None, tq, 256), lambda bi, i: (bi, i, CB_IDX)),
                  pl.BlockSpec((None, B_HEADS, s_pad, B_DIM), lambda bi, i: (bi, 0, 0, 0)),
                  pl.BlockSpec((None, B_HEADS, s_pad, B_DIM), lambda bi, i: (bi, 0, 0, 0)),
                  pl.BlockSpec((None, s_pad, LANES), lambda bi, i: (bi, 0, 0)),
                  pl.BlockSpec((3, B_HEADS, LANES, LANES), lambda bi, i: (0, 0, 0, 0))],
        out_specs=pl.BlockSpec((None, tq, 256), lambda bi, i: (bi, i, 0)),
        scratch_shapes=[pltpu.VMEM((tq, s_pad), jnp.int32), pltpu.VMEM((s_pad, tqs), jnp.int32),
                        pltpu.VMEM((tq, s_pad), F32)],
        compiler_params=_cparams(("parallel", "arbitrary")),
        name="dsa_attn",
    )(proj3, proj3, bk, bv, ki, bias)


def _pool_kernel(u_ref, prev_ref, w_ref, scale_ref, o_ref, st_ref, ext_ref, *, tr, past):
    i = pl.program_id(1)

    @pl.when(i == 0)
    def _():
        ext_ref[1:16, :] = prev_ref[...]

    u = u_ref[...]
    ext_ref[16:16 + tr, :] = u

    def tap(k, lo):
        return ext_ref[16 - k:16 - k + tr, lo:lo + LANES]

    pos1 = past + i * tr + 1 + lax.broadcasted_iota(jnp.int32, (tr, 1), 0)
    cnt = [jnp.minimum(w, pos1).astype(F32) for w in POOL_WINDOWS]
    lane = lax.broadcasted_iota(jnp.int32, (1, LANES), 1)
    first = lane < POOL_GROUP

    s2 = tap(0, 0) + tap(1, 0)
    s4 = s2 + tap(2, 0) + tap(3, 0)
    s8 = tap(0, LANES)
    for k in range(1, 8):
        s8 = s8 + tap(k, LANES)
    s16 = s8
    for k in range(8, 16):
        s16 = s16 + tap(k, LANES)
    m_lo = jnp.where(first, s2 / cnt[0], s4 / cnt[1])
    m_hi = jnp.where(first, s8 / cnt[2], s16 / cnt[3])
    pooled = jnp.concatenate([m_lo, m_hi], axis=1) - u
    mixed = jnp.dot(pooled.astype(BF16), w_ref[...], preferred_element_type=F32)
    o_ref[...] = mixed * scale_ref[...]

    new_state = ext_ref[1 + tr:16 + tr, :]
    st_ref[...] = new_state
    ext_ref[1:16, :] = new_state


def _pool(proj3, prev, w_bd, scale, tr, past):
    b, t, _ = proj3.shape
    kern = functools.partial(_pool_kernel, tr=tr, past=past)
    return pl.pallas_call(
        kern,
        out_shape=(jax.ShapeDtypeStruct((b, t, POOL_WIDTH), F32),
                   jax.ShapeDtypeStruct((b, POOL_STATE, POOL_WIDTH), F32)),
        grid=(b, t // tr),
        in_specs=[pl.BlockSpec((None, tr, 256), lambda bi, i: (bi, i, CB_POOL)),
                  pl.BlockSpec((None, POOL_STATE, POOL_WIDTH), lambda bi, i: (bi, 0, 0)),
                  pl.BlockSpec((256, 256), lambda bi, i: (0, 0)),
                  pl.BlockSpec((1, 256), lambda bi, i: (0, 0))],
        out_specs=(pl.BlockSpec((None, tr, 256), lambda bi, i: (bi, i, 0)),
                   pl.BlockSpec((None, POOL_STATE, POOL_WIDTH), lambda bi, i: (bi, 0, 0))),
        scratch_shapes=[pltpu.VMEM((tr + 16, POOL_WIDTH), F32)],
        compiler_params=_cparams(("parallel", "arbitrary")),
        name="pool_mix",
    )(proj3, prev, w_bd, scale)


def _conv_kernel(u_ref, prev_ref, w_ref, b_ref, g_ref, beta_ref, o_ref, st_ref, ext_ref, *, tr, sub):
    i = pl.program_id(1)
    npre = CONV_K - 1

    @pl.when(i == 0)
    def _():
        ext_ref[2:2 + npre, :] = prev_ref[...]

    u = u_ref[...]
    ext_ref[32:32 + tr, :] = u[:, :CONV_WIDTH] * _sigmoid(u[:, CONV_WIDTH:])

    for r0 in range(0, tr, sub):
        acc = ext_ref[r0 + 2:r0 + 2 + sub, :] * w_ref[0:1, :]
        for k in range(1, CONV_K):
            acc = acc + ext_ref[r0 + 2 + k:r0 + 2 + k + sub, :] * w_ref[k:k + 1, :]
        y = _ln_rows(acc + b_ref[...], g_ref[...], beta_ref[...])
        o_ref[r0:r0 + sub, :] = y * _sigmoid(y)

    new_state = ext_ref[2 + tr:2 + tr + npre, :]
    st_ref[...] = new_state
    ext_ref[2:2 + npre, :] = new_state


def _conv(proj3, prev, w, bias, g, beta, tr, sub):
    b, t, _ = proj3.shape
    npre = CONV_K - 1
    kern = functools.partial(_conv_kernel, tr=tr, sub=sub)
    vec = pl.BlockSpec((1, CONV_WIDTH), lambda bi, i: (0, 0))
    return pl.pallas_call(
        kern,
        out_shape=(jax.ShapeDtypeStruct((b, t, CONV_WIDTH), F32),
                   jax.ShapeDtypeStruct((b, npre, CONV_WIDTH), F32)),
        grid=(b, t // tr),
        in_specs=[pl.BlockSpec((None, tr, 512), lambda bi, i: (bi, i, CB_CONV)),
                  pl.BlockSpec((None, npre, CONV_WIDTH), lambda bi, i: (bi, 0, 0)),
                  pl.BlockSpec((CONV_K, CONV_WIDTH), lambda bi, i: (0, 0)),
                  vec, vec, vec],
        out_specs=(pl.BlockSpec((None, tr, CONV_WIDTH), lambda bi, i: (bi, i, 0)),
                   pl.BlockSpec((None, npre, CONV_WIDTH), lambda bi, i: (bi, 0, 0))),
        scratch_shapes=[pltpu.VMEM((tr + 32, CONV_WIDTH), F32)],
        compiler_params=_cparams(("parallel", "arbitrary")),
        name="conv_module",
    )(proj3, prev, w, bias, g, beta)


def _merge_kernel(a_ref, b_ref, c_ref, d_ref, gates_ref, x_ref, wb_ref, wo_ref, g_ref, beta_ref,
                  o_ref):
    mixed = None
    for n, br_ref in enumerate((a_ref, b_ref, c_ref, d_ref)):
        br = jnp.dot(br_ref[...].astype(BF16), wb_ref[n], preferred_element_type=F32)
        term = _sigmoid(gates_ref[:, n * D_MODEL:(n + 1) * D_MODEL]) * br
        mixed = term if mixed is None else mixed + term
    y = jnp.dot(mixed.astype(BF16), wo_ref[...], preferred_element_type=F32)
    o_ref[...] = _ln_rows(ALPHA * x_ref[...] + y, g_ref[...], beta_ref[...])


def _merge(oa, ob, oc, od, proj3, x3, wb, wo, g, beta, tm):
    b, t, _ = x3.shape
    br_spec = pl.BlockSpec((None, tm, 256), lambda bi, i: (bi, i, 0))
    vec = pl.BlockSpec((1, D_MODEL), lambda bi, i: (0, 0))
    return pl.pallas_call(
        _merge_kernel,
        out_shape=jax.ShapeDtypeStruct((b, t, D_MODEL), F32),
        grid=(b, t // tm),
        in_specs=[br_spec, br_spec, br_spec, br_spec,
                  pl.BlockSpec((None, tm, N_BRANCH * D_MODEL), lambda bi, i: (bi, i, CB_GATES)),
                  pl.BlockSpec((None, tm, D_MODEL), lambda bi, i: (bi, i, 0)),
                  pl.BlockSpec((N_BRANCH, 256, D_MODEL), lambda bi, i: (0, 0, 0)),
                  pl.BlockSpec((D_MODEL, D_MODEL), lambda bi, i: (0, 0)),
                  vec, vec],
        out_specs=pl.BlockSpec((None, tm, D_MODEL), lambda bi, i: (bi, i, 0)),
        compiler_params=_cparams(("parallel", "parallel")),
        name="merge_ln1",
    )(oa, ob, oc, od, proj3, x3, wb, wo, g, beta)


FF_TC = 256
FF_NC = D_FF // FF_TC


def _ffn_kernel(x_ref, st_ref, wup_ref, cw_ref, cb_ref, wd_ref, g_ref, beta_ref, o_ref, nst_ref,
                car_ref, ubuf_ref, h_ref, *, tm):
    @pl.when(pl.program_id(1) == 0)
    def _():
        car_ref[6:8, :] = st_ref[...]

    x = x_ref[...]
    xb = x.astype(BF16)

    def conv3(off, slot):
        cols = slice(off, off + FF_TC)
        u = jnp.dot(xb, wup_ref[:, cols], preferred_element_type=F32)
        ubuf_ref[slot, 6:8, :] = car_ref[6:8, cols]
        ubuf_ref[slot, 8:8 + tm, :] = u
        y = (cw_ref[0:1, cols] * ubuf_ref[slot, 6:6 + tm, :]
             + cw_ref[1:2, cols] * ubuf_ref[slot, 7:7 + tm, :]
             + cw_ref[2:3, cols] * u + cb_ref[:, cols])
        car_ref[6:8, cols] = ubuf_ref[slot, 6 + tm:8 + tm, :]
        return y

    for c in range(FF_NC):
        val = conv3(c * FF_TC, (2 * c) % 4)
        gate = conv3(D_FF + c * FF_TC, (2 * c + 1) % 4)
        h_ref[:, c * FF_TC:(c + 1) * FF_TC] = ((gate * _sigmoid(gate)) * val).astype(BF16)

    nst_ref[...] = car_ref[6:8, :]
    y = jnp.dot(h_ref[...], wd_ref[...], preferred_element_type=F32)
    o_ref[...] = _ln_rows(ALPHA * x + y, g_ref[...], beta_ref[...])


def _ffn(x3, st, w_up, cw, cb, w_dn, g, beta, tm):
    b, t, _ = x3.shape
    kern = functools.partial(_ffn_kernel, tm=tm)
    nk = FFN_K - 1
    const = lambda shape: pl.BlockSpec(shape, lambda bi, i: (0,) * len(shape),
                                       pipeline_mode=pl.Buffered(1))
    return pl.pallas_call(
        kern,
        out_shape=(jax.ShapeDtypeStruct((b, t, D_MODEL), F32),
                   jax.ShapeDtypeStruct((b, nk, 2 * D_FF), F32)),
        grid=(b, t // tm),
        in_specs=[pl.BlockSpec((None, tm, D_MODEL), lambda bi, i: (bi, i, 0)),
                  pl.BlockSpec((None, nk, 2 * D_FF), lambda bi, i: (bi, 0, 0)),
                  const((D_MODEL, 2 * D_FF)), const((FFN_K, 2 * D_FF)), const((1, 2 * D_FF)),
                  const((D_FF, D_MODEL)), const((1, D_MODEL)), const((1, D_MODEL))],
        out_specs=(pl.BlockSpec((None, tm, D_MODEL), lambda bi, i: (bi, i, 0)),
                   pl.BlockSpec((None, nk, 2 * D_FF), lambda bi, i: (bi, 0, 0))),
        scratch_shapes=[pltpu.VMEM((8, 2 * D_FF), F32),
                        pltpu.VMEM((4, tm + 8, FF_TC), F32),
                        pltpu.VMEM((tm, D_FF), BF16)],
        compiler_params=_cparams(("parallel", "arbitrary")),
        name="conv_ffn_ln2",
    )(x3, st, w_up, cw, cb, w_dn, g, beta)


def _rope_tables(past, t):
    half = A_ROPE // 2
    freqs = ROPE_BASE ** (-jnp.arange(half, dtype=F32) / half)
    pos = past + jnp.arange(t, dtype=jnp.int32)
    ang = pos.astype(F32)[:, None] * freqs[None, :]
    cos, sin = jnp.cos(ang), jnp.sin(ang)
    c2 = jnp.concatenate([cos, cos], axis=1)
    s2 = jnp.concatenate([sin, sin], axis=1)
    one = jnp.ones((t, A_NOPE), F32)
    z = lambda n: jnp.zeros((t, n), F32)
    cosq = jnp.concatenate([one, c2, z(LANES - A_NOPE - A_ROPE)], axis=1)
    sinq = jnp.concatenate([z(A_NOPE), s2, z(LANES - A_NOPE - A_ROPE)], axis=1)
    cosk = jnp.concatenate([c2, z(LANES - A_ROPE)], axis=1)
    sink = jnp.concatenate([s2, z(LANES - A_ROPE)], axis=1)
    return cosq, sinq, cosk, sink


_BUCKET_EDGES = (12, 16, 23, 32, 46, 64, 91)


def _rel_bucket(rel):
    nb = REL_BUCKETS // 2
    max_exact = nb // 2
    n = np.abs(rel)
    large = max_exact + sum((n >= e).astype(np.int64) for e in _BUCKET_EDGES)
    return np.where(rel > 0, nb, 0) + np.where(n < max_exact, n, large)


def _bias_tiles(rel_bias):
    r = np.arange(LANES)[:, None]
    c = np.arange(LANES)[None, :]
    buckets = np.stack([_rel_bucket(c - r - LANES * d) for d in range(3)])
    tiles = jnp.zeros((B_HEADS,) + buckets.shape, F32)
    for bkt in np.unique(buckets):
        sel = jnp.asarray(buckets == bkt)
        tiles = jnp.where(sel[None], rel_bias[int(bkt)][:, None, None, None], tiles)
    return jnp.transpose(tiles, (1, 0, 2, 3))


def _pack_w_in(w):
    z = lambda n: jnp.zeros((D_MODEL, n), F32)
    kr = w[:, O_KROPE:O_KROPE + A_ROPE]
    half = A_ROPE // 2
    kr_rot = jnp.concatenate([-kr[:, half:], kr[:, :half]], axis=1)
    parts = [
        w[:, O_GATES:O_GATES + N_BRANCH * D_MODEL],
        w[:, O_CQ:O_CQ + A_Q_LORA], z(256 - A_Q_LORA),
        w[:, O_CKV:O_CKV + A_KV_LORA], kr, kr_rot, z(256 - A_KV_LORA - 2 * A_ROPE),
        w[:, O_BQ:O_BQ + 256], w[:, O_BK:O_BK + 256], w[:, O_BV:O_BV + 256],
        w[:, O_QIDX:O_QIDX + 128], w[:, O_KIDX:O_KIDX + IDX_DIM], w[:, O_WIDX:O_WIDX + IDX_HEADS],
        z(256 - 128 - IDX_DIM - IDX_HEADS),
        w[:, O_CONV:O_CONV + 2 * CONV_WIDTH],
        w[:, O_POOL:O_POOL + POOL_WIDTH],
    ]
    return jnp.concatenate(parts, axis=1).astype(BF16)


def _pack_w_qup(w):
    zrow = lambda m: jnp.zeros((256 - A_Q_LORA, m.shape[1]), F32)
    half = A_ROPE // 2
    dq = A_NOPE + A_ROPE
    cols, rcols = [], []
    for h in range(A_HEADS):
        nope = w[:, h * dq:h * dq + A_NOPE]
        rp = w[:, h * dq + A_NOPE:(h + 1) * dq]
        pad = jnp.zeros((A_Q_LORA, LANES - dq), F32)
        cols += [nope, rp, pad]
        rcols += [jnp.zeros_like(nope), -rp[:, half:], rp[:, :half], pad]
    wq = jnp.concatenate(cols, axis=1)
    wqr = jnp.concatenate(rcols, axis=1)
    wq = jnp.concatenate([wq, zrow(wq)], axis=0).astype(BF16)
    wqr = jnp.concatenate([wqr, zrow(wqr)], axis=0).astype(BF16)
    return wq, wqr


def _pack_w_kvup(w):
    dk = A_NOPE + A_V
    kc, vc = [], []
    for h in range(A_HEADS):
        kc += [w[:, h * dk:h * dk + A_NOPE], jnp.zeros((A_KV_LORA, LANES - A_NOPE), F32)]
        vc += [w[:, h * dk + A_NOPE:(h + 1) * dk]]
    return jnp.concatenate(kc, axis=1).astype(BF16), jnp.concatenate(vc, axis=1).astype(BF16)


def _rope_placement():
    e = np.zeros((LANES, A_HEADS * LANES), np.float32)
    for h in range(A_HEADS):
        for r in range(A_ROPE):
            e[r, h * LANES + A_NOPE + r] = 1.0
    return jnp.asarray(e, dtype=BF16)


def _pool_blockdiag(pw):
    out = jnp.zeros((POOL_WIDTH, POOL_WIDTH), F32)
    for g in range(len(POOL_WINDOWS)):
        out = out.at[g * POOL_GROUP:(g + 1) * POOL_GROUP, g * POOL_GROUP:(g + 1) * POOL_GROUP].set(pw[g])
    return out.astype(BF16)


class _Tiles:
    def __init__(self, t):
        prompt = t > 128
        self.rows = 1024 if prompt else 128
        self.in_tn = 1280
        self.prep = 512 if prompt else t
        self.kv = 512
        self.mla_q = 256 if prompt else t
        self.mla_k = 512 if prompt else 17 * LANES
        self.dsa_q = 256 if prompt else t
        self.dsa_kc = 512 if prompt else 17 * LANES
        self.pool = 512 if prompt else t
        self.conv = 256 if prompt else t
        self.conv_sub = 64 if prompt else t
        self.merge = 512 if prompt else t
        self.ffn = 512 if prompt else t
        self.s_align = 512 if prompt else 17 * LANES


def _trunk_layer(x3, caches, lw, consts, past):
    (w_in, gq, gkv, wq, wqr, wkn, wv, pool_bd, pool_scale, conv_w, conv_b, conv_g, conv_beta,
     w_branch, w_out, ln1_g, ln1_b, w_up, ffn_cw, ffn_cb, w_down, ln2_g, ln2_b) = lw
    tabs, bias, e_place = consts
    b, t, _ = x3.shape
    tl = _Tiles(t)
    s_valid = past + t
    s_pad = -(-s_valid // tl.s_align) * tl.s_align
    topk = min(TOPK_MAX, s_valid // 4)

    proj = _inproj(x3.reshape(b * t, D_MODEL), w_in, tl.rows, tl.in_tn)
    proj3 = proj.reshape(b, t, PROJ_W)
    qa, ckv_n, kr128 = _aprep(proj3, tabs, gq, gkv, wq, wqr, tl.prep)

    col = lambda blk, off, n: proj3[:, :, blk * 256 + off:blk * 256 + off + n]
    bk_new = col(CB_BK, 0, 256)
    bv_new = col(CB_BV, 0, 256)
    kidx_new = col(CB_IDX, 128, IDX_DIM)
    krope_new = kr128[:, :, :A_ROPE]

    def keys(cache, new, width):
        parts = [new] if cache is None else [cache.reshape(b, past, -1), new]
        if s_pad > s_valid:
            parts.append(jnp.zeros((b, s_pad - s_valid, width), new.dtype))
        return parts[0] if len(parts) == 1 else jnp.concatenate(parts, axis=1)

    c_ckv, c_krope, c_bk, c_bv, c_kidx, st_pool, st_conv, st_ffn = caches
    ckv_all = keys(c_ckv, ckv_n, A_KV_LORA)
    if c_krope is None:
        kr_all = keys(None, kr128, LANES)
    else:
        c_kr128 = jnp.pad(c_krope, ((0, 0), (0, 0), (0, LANES - A_ROPE)))
        kr_all = keys(c_kr128, kr128, LANES)
    head_major = lambda a: jnp.transpose(a.astype(BF16).reshape(b, s_pad, B_HEADS, B_DIM),
                                         (0, 2, 1, 3))
    bk_all = head_major(keys(c_bk, bk_new, 256))
    bv_all = head_major(keys(c_bv, bv_new, 256))
    ki_all = jnp.tile(keys(c_kidx, kidx_new, IDX_DIM).astype(BF16), (1, 1, IDX_HEADS))

    kx, vx = _kvup(ckv_all.reshape(b * s_pad, A_KV_LORA), kr_all.reshape(b * s_pad, LANES),
                   wkn, wv, e_place, tl.kv)
    out_a = _mla(qa, kx.reshape(b, s_pad, 512), vx.reshape(A_HEADS, b, s_pad, A_V),
                 tl.mla_q, tl.mla_k, past, s_valid)
    out_b = _dsa(proj3, bk_all, bv_all, ki_all, bias, tl.dsa_q, tl.dsa_kc, past, s_valid, topk)
    out_c, new_pool = _pool(proj3, st_pool, pool_bd, pool_scale, tl.pool, past)
    out_d, new_conv = _conv(proj3, st_conv, conv_w, conv_b, conv_g, conv_beta, tl.conv, tl.conv_sub)
    x1 = _merge(out_a, out_b, out_c, out_d, proj3, x3, w_branch, w_out, ln1_g, ln1_b, tl.merge)
    x2, new_ffn = _ffn(x1, st_ffn, w_up, ffn_cw, ffn_cb, w_down, ln2_g, ln2_b, tl.ffn)
    states = (ckv_n, krope_new, bk_new.reshape(b, t, B_HEADS, B_DIM),
              bv_new.reshape(b, t, B_HEADS, B_DIM), kidx_new, new_pool, new_conv, new_ffn)
    return x2, states


def kernel(x_prompt, x_sample, cache_a_ckv, cache_a_krope, cache_b_k, cache_b_v, cache_b_kidx,
           state_pool, state_conv, state_ffn, rel_bias, ln_in_g, ln_in_b, w_in, a_q_norm, a_kv_norm,
           a_w_qup, a_w_kvup, pool_w, pool_scale, conv_w, conv_b, conv_ln_g, conv_ln_b, w_branch,
           w_out, ln1_g, ln1_b, w_up, ffn_conv_w, ffn_conv_b, w_down, ln2_g, ln2_b):
    bp, tp, _ = x_prompt.shape
    bs, ts, _ = x_sample.shape
    past = cache_a_ckv.shape[2]
    depth = w_in.shape[0]

    xp = _layer_norm(x_prompt.reshape(bp * tp, D_MODEL), ln_in_g, ln_in_b,
                     _Tiles(tp).rows).reshape(bp, tp, D_MODEL)
    xs = _layer_norm(x_sample.reshape(bs * ts, D_MODEL), ln_in_g, ln_in_b,
                     _Tiles(ts).rows).reshape(bs, ts, D_MODEL)

    e_place = _rope_placement()
    bias = _bias_tiles(rel_bias)
    consts_p = (_rope_tables(0, tp), bias, e_place)
    consts_s = (_rope_tables(past, ts), bias, e_place)
    zeros_p = (None, None, None, None, None,
               jnp.zeros((bp, POOL_STATE, POOL_WIDTH), F32),
               jnp.zeros((bp, CONV_K - 1, CONV_WIDTH), F32),
               jnp.zeros((bp, FFN_K - 1, 2 * D_FF), F32))

    row = lambda v: v.reshape(1, -1)
    p_states, s_states = [], []
    for l in range(depth):
        wq, wqr = _pack_w_qup(a_w_qup[l])
        wkn, wv = _pack_w_kvup(a_w_kvup[l])
        gq = jnp.concatenate([a_q_norm[l], jnp.zeros((256 - A_Q_LORA,), F32)]).reshape(1, 256)
        lw = (_pack_w_in(w_in[l]), gq, row(a_kv_norm[l]), wq, wqr, wkn, wv,
              _pool_blockdiag(pool_w[l]), row(pool_scale[l]), conv_w[l], row(conv_b[l]),
              row(conv_ln_g[l]), row(conv_ln_b[l]), w_branch[l].astype(BF16), w_out[l].astype(BF16),
              row(ln1_g[l]), row(ln1_b[l]), w_up[l].astype(BF16), ffn_conv_w[l], row(ffn_conv_b[l]),
              w_down[l].astype(BF16), row(ln2_g[l]), row(ln2_b[l]))
        xp, st_p = _trunk_layer(xp, zeros_p, lw, consts_p, 0)
        caches_s = (cache_a_ckv[l], cache_a_krope[l], cache_b_k[l], cache_b_v[l], cache_b_kidx[l],
                    state_pool[l], state_conv[l], state_ffn[l])
        xs, st_s = _trunk_layer(xs, caches_s, lw, consts_s, past)
        p_states.append(st_p)
        s_states.append(st_s)

    p_out = [jnp.stack(a) for a in zip(*p_states)]
    s_out = [jnp.stack(a) for a in zip(*s_states)]
    return (xp, xs, *p_out, *s_out)
```

```python
import functools

import numpy as np
import jax
import jax.numpy as jnp
from jax import lax
from jax.experimental import pallas as pl
from jax.experimental.pallas import tpu as pltpu

F32 = jnp.float32
BF16 = jnp.bfloat16

D_MODEL = 1024
DEPTH = 4
CHUNK = 64
CHUNK_SHIFT = 6
N_BRANCH = 4
A_HEADS = 4
A_NOPE = 64
A_ROPE = 32
A_V = 64
A_Q_LORA = 192
A_KV_LORA = 128
A_SCALE = (A_NOPE + A_ROPE) ** -0.5
ROPE_BASE = 10000.0
B_HEADS = 4
B_DIM = 64
B_SCALE = B_DIM ** -0.5
IDX_HEADS = 4
IDX_DIM = 32
IDX_SCALE = (IDX_HEADS ** -0.5) * (IDX_DIM ** -0.5)
TOPK_MAX = 256
REL_BUCKETS = 32
POOL_WINDOWS = (2, 4, 8, 16)
POOL_WIDTH = 256
POOL_GROUP = 64
POOL_STATE = 15
CONV_WIDTH = 256
CONV_K = 31
D_FF = 2816
FFN_K = 3
ALPHA = (2 * DEPTH) ** 0.25
LN_EPS = 1e-5

IN_SPLITS = (A_Q_LORA, A_KV_LORA, A_ROPE, 256, 256, 256, 128, IDX_DIM, IDX_HEADS,
             POOL_WIDTH, 2 * CONV_WIDTH, N_BRANCH * D_MODEL)
_OFF = [0] + [int(v) for v in np.cumsum(IN_SPLITS)]
(O_CQ, O_CKV, O_KROPE, O_BQ, O_BK, O_BV, O_QIDX, O_KIDX, O_WIDX, O_POOL, O_CONV, O_GATES,
 D_IN) = _OFF

PROJ_W = 2304
CB_CONV = 0
CB_CQ = 2
CB_CK = 3
CB_BQ = 4
CB_BK = 5
CB_BV = 6
CB_IDX = 7
CB_POOL = 8

VMEM_LIMIT_BYTES = 48 * 1024 * 1024
LANES = 128

NEG = -1e30
M_INIT = -5e29
SEARCH_MAX_IT = 320
SNAP_START = 14
SNAP_EVERY = 4

_NT = (((1,), (1,)), ((), ()))


def _cparams(sem):
    return pltpu.CompilerParams(dimension_semantics=sem, vmem_limit_bytes=VMEM_LIMIT_BYTES)


def _ln_rows(x, g, b):
    mu = jnp.mean(x, axis=-1, keepdims=True)
    xc = x - mu
    var = jnp.mean(xc * xc, axis=-1, keepdims=True)
    return xc * lax.rsqrt(var + LN_EPS) * g + b


def _sigmoid(x):
    return 1.0 / (1.0 + jnp.exp(-x))


def _ln_kernel(x_ref, g_ref, b_ref, o_ref):
    o_ref[...] = _ln_rows(x_ref[...], g_ref[...], b_ref[...])


def _layer_norm(x2d, g, b, tm):
    n, d = x2d.shape
    return pl.pallas_call(
        _ln_kernel,
        out_shape=jax.ShapeDtypeStruct((n, d), F32),
        grid=(n // tm,),
        in_specs=[pl.BlockSpec((tm, d), lambda i: (i, 0)),
                  pl.BlockSpec((1, d), lambda i: (0, 0)),
                  pl.BlockSpec((1, d), lambda i: (0, 0))],
        out_specs=pl.BlockSpec((tm, d), lambda i: (i, 0)),
        compiler_params=_cparams(("parallel",)),
        name="input_ln",
    )(x2d, g.reshape(1, d), b.reshape(1, d))


def _inproj_kernel(x_ref, w_ref, o_ref, xb_ref):
    @pl.when(pl.program_id(1) == 0)
    def _():
        xb_ref[...] = x_ref[...].astype(BF16)

    o_ref[...] = jnp.dot(xb_ref[...], w_ref[...], preferred_element_type=F32)


def _inproj(x2d, w, tm, tn):
    n, k = x2d.shape
    nw = w.shape[1]
    return pl.pallas_call(
        _inproj_kernel,
        out_shape=jax.ShapeDtypeStruct((n, nw), F32),
        grid=(n // tm, nw // tn),
        in_specs=[pl.BlockSpec((tm, k), lambda i, j: (i, 0)),
                  pl.BlockSpec((k, tn), lambda i, j: (0, j))],
        out_specs=pl.BlockSpec((tm, tn), lambda i, j: (i, j)),
        scratch_shapes=[pltpu.VMEM((tm, k), BF16)],
        compiler_params=_cparams(("parallel", "arbitrary")),
        name="in_proj",
    )(x2d, w)


def _aprep_kernel(cq_ref, ck_ref, bk_ref, bv_ref, idx_ref, cosq_ref, sinq_ref, cosk_ref, sink_ref,
                  gq_ref, gkv_ref, wq_ref, wqr_ref, _ckv_in, _kro_in, _sbk_in, _sbv_in, _ski_in,
                  qa_ref, kr_ref, hbk_ref, hbv_ref, rki_ref,
                  ckv_ref, kro_ref, sbk_ref, sbv_ref, ski_ref):
    bk = bk_ref[...]
    bv = bv_ref[...]
    sbk_ref[...] = bk
    sbv_ref[...] = bv
    bkb = bk.astype(BF16)
    bvb = bv.astype(BF16)
    for h in range(B_HEADS):
        hbk_ref[h] = bkb[:, h * B_DIM:(h + 1) * B_DIM]
        hbv_ref[h] = bvb[:, h * B_DIM:(h + 1) * B_DIM]
    ki = idx_ref[:, LANES:]
    ski_ref[...] = ki[:, :IDX_DIM]
    lane = lax.broadcasted_iota(jnp.int32, (1, LANES), 1)
    k0 = jnp.where(lane < IDX_DIM, ki, 0.0)
    rep = k0
    for h in range(1, IDX_HEADS):
        rep = rep + pltpu.roll(k0, h * IDX_DIM, 1)
    rki_ref[...] = rep.astype(BF16)

    cq = cq_ref[...]
    ms = jnp.sum(cq * cq, axis=-1, keepdims=True) * (1.0 / A_Q_LORA)
    cqb = ((cq * lax.rsqrt(ms + LN_EPS)) * gq_ref[...]).astype(BF16)
    q = jnp.dot(cqb, wq_ref[...], preferred_element_type=F32)
    qr = jnp.dot(cqb, wqr_ref[...], preferred_element_type=F32)
    cos = jnp.concatenate([cosq_ref[...]] * A_HEADS, axis=1)
    sin = jnp.concatenate([sinq_ref[...]] * A_HEADS, axis=1)
    qa_ref[...] = ((q * cos + qr * sin) * A_SCALE).astype(BF16)

    ck = ck_ref[...]
    ckv = ck[:, :A_KV_LORA]
    ms2 = jnp.mean(ckv * ckv, axis=-1, keepdims=True)
    ckv_ref[...] = (ckv * lax.rsqrt(ms2 + LN_EPS)) * gkv_ref[...]
    kr = ck[:, A_KV_LORA:]
    krr = pltpu.roll(kr, LANES - A_ROPE, 1)
    kr_new = kr * cosk_ref[...] + krr * sink_ref[...]
    kr_ref[...] = kr_new
    kro_ref[...] = kr_new[:, :A_ROPE]


def _aprep(proj3, tabs, gq, gkv, wq, wqr, stacks, layer, tm):
    b, t, _ = proj3.shape
    cosq, sinq, cosk, sink = tabs
    tab_spec = pl.BlockSpec((tm, LANES), lambda bi, i: (i, 0))
    full = lambda shape: pl.BlockSpec(shape, lambda bi, i: (0,) * len(shape))
    blk = lambda cb: pl.BlockSpec((None, tm, 256), lambda bi, i: (bi, i, cb))
    rows = lambda w: pl.BlockSpec((None, tm, w), lambda bi, i: (bi, i, 0))
    heads = pl.BlockSpec((None, B_HEADS, tm, B_DIM), lambda bi, i: (bi, 0, i, 0))
    stacked = lambda a: pl.BlockSpec((None, None, tm, a.shape[-1]), lambda bi, i: (layer, bi, i, 0))
    untouched = pl.BlockSpec(memory_space=pl.ANY)
    n_in = 13
    out = pl.pallas_call(
        _aprep_kernel,
        out_shape=(jax.ShapeDtypeStruct((b, t, 512), BF16),
                   jax.ShapeDtypeStruct((b, t, LANES), F32),
                   jax.ShapeDtypeStruct((b, B_HEADS, t, B_DIM), BF16),
                   jax.ShapeDtypeStruct((b, B_HEADS, t, B_DIM), BF16),
                   jax.ShapeDtypeStruct((b, t, LANES), BF16))
        + tuple(jax.ShapeDtypeStruct(a.shape, a.dtype) for a in stacks),
        grid=(b, t // tm),
        in_specs=[blk(CB_CQ), blk(CB_CK), blk(CB_BK), blk(CB_BV), blk(CB_IDX),
                  tab_spec, tab_spec, tab_spec, tab_spec,
                  full((1, 256)), full((1, A_KV_LORA)), full((256, 512)), full((256, 512))]
        + [untouched] * len(stacks),
        out_specs=(rows(512), rows(LANES), heads, heads, rows(LANES))
        + tuple(stacked(a) for a in stacks),
        input_output_aliases={n_in + k: 5 + k for k in range(len(stacks))},
        compiler_params=_cparams(("parallel", "parallel")),
        name="mla_prep",
    )(proj3, proj3, proj3, proj3, proj3, cosq, sinq, cosk, sink, gq, gkv, wq, wqr, *stacks)
    return out[:5], tuple(out[5:])


def _kvup_kernel(ckv_ref, kr_ref, wkn_ref, wv_ref, e_ref, k_ref, v_ref):
    cb = ckv_ref[...].astype(BF16)
    k = jnp.dot(cb, wkn_ref[...], preferred_element_type=F32)
    k = k + jnp.dot(kr_ref[...].astype(BF16), e_ref[...], preferred_element_type=F32)
    k_ref[...] = k.astype(BF16)
    v = jnp.dot(cb, wv_ref[...], preferred_element_type=F32).astype(BF16)
    for h in range(A_HEADS):
        v_ref[h] = v[:, h * A_V:(h + 1) * A_V]


def _kvup(ckv2d, kr2d, wkn, wv, e, tm):
    m = ckv2d.shape[0]
    full = lambda shape: pl.BlockSpec(shape, lambda i: (0,) * len(shape))
    return pl.pallas_call(
        _kvup_kernel,
        out_shape=(jax.ShapeDtypeStruct((m, 512), BF16),
                   jax.ShapeDtypeStruct((A_HEADS, m, A_V), BF16)),
        grid=(m // tm,),
        in_specs=[pl.BlockSpec((tm, A_KV_LORA), lambda i: (i, 0)),
                  pl.BlockSpec((tm, LANES), lambda i: (i, 0)),
                  full((A_KV_LORA, 512)), full((A_KV_LORA, 256)), full((LANES, 512))],
        out_specs=(pl.BlockSpec((tm, 512), lambda i: (i, 0)),
                   pl.BlockSpec((A_HEADS, tm, A_V), lambda i: (0, i, 0))),
        compiler_params=_cparams(("parallel",)),
        name="kv_up",
    )(ckv2d, kr2d, wkn, wv, e)


def _online(carry, s, v):
    m, l, acc = carry
    m_new = jnp.maximum(m, jnp.max(s, axis=1, keepdims=True))
    alpha = jnp.exp(m - m_new)
    p = jnp.exp(s - m_new)
    l = alpha * l + jnp.sum(p, axis=1, keepdims=True)
    acc = alpha * acc + jnp.dot(p.astype(BF16), v, preferred_element_type=F32)
    return m_new, l, acc


def _softmax_init(tq, dv):
    return (jnp.full((tq, 1), M_INIT, F32), jnp.zeros((tq, 1), F32), jnp.zeros((tq, dv), F32))


def _visible_limit(qpos, s_valid):
    return jnp.minimum((jnp.right_shift(qpos, CHUNK_SHIFT) + 1) * CHUNK, s_valid)


def _mla_kernel(q_ref, k_ref, v_ref, o_ref, *, tq, tk, past, s_valid):
    i = pl.program_id(1)
    q0 = past + i * tq
    qpos = q0 + lax.broadcasted_iota(jnp.int32, (tq, 1), 0)
    q_lim = _visible_limit(qpos, s_valid)
    lim_first = _visible_limit(q0, s_valid)
    lim_last = _visible_limit(q0 + tq - 1, s_valid)
    n_full = lim_first // tk
    n_tot = (lim_last + tk - 1) // tk
    q = q_ref[...]
    qhs = [q[:, h * LANES:(h + 1) * LANES] for h in range(A_HEADS)]

    def body(j, carries, masked):
        ks = pl.ds(pl.multiple_of(j * tk, tk), tk)
        if masked:
            kpos = j * tk + lax.broadcasted_iota(jnp.int32, (1, tk), 1)
            vis = kpos < q_lim
        out = []
        for h in range(A_HEADS):
            s = lax.dot_general(qhs[h], k_ref[ks, h * LANES:(h + 1) * LANES], _NT,
                                preferred_element_type=F32)
            if masked:
                s = jnp.where(vis, s, NEG)
            out.append(_online(carries[h], s, v_ref[h, ks, :]))
        return tuple(out)

    init = tuple(_softmax_init(tq, A_V) for _ in range(A_HEADS))
    carries = lax.fori_loop(0, n_full, functools.partial(body, masked=False), init)
    carries = lax.fori_loop(n_full, n_tot, functools.partial(body, masked=True), carries)
    for h in range(A_HEADS):
        _, l, acc = carries[h]
        o_ref[:, h * A_V:(h + 1) * A_V] = acc / l


def _mla(qa, kx, v, tq, tk, past, s_valid):
    b, t, _ = qa.shape
    s_pad = kx.shape[1]
    kern = functools.partial(_mla_kernel, tq=tq, tk=tk, past=past, s_valid=s_valid)
    return pl.pallas_call(
        kern,
        out_shape=jax.ShapeDtypeStruct((b, t, 256), F32),
        grid=(b, t // tq),
        in_specs=[pl.BlockSpec((None, tq, 512), lambda bi, i: (bi, i, 0)),
                  pl.BlockSpec((None, s_pad, 512), lambda bi, i: (bi, 0, 0)),
                  pl.BlockSpec((A_HEADS, None, s_pad, A_V), lambda bi, i: (0, bi, 0, 0))],
        out_specs=pl.BlockSpec((None, tq, 256), lambda bi, i: (bi, i, 0)),
        compiler_params=_cparams(("parallel", "arbitrary")),
        name="mla_attn",
    )(qa, kx, v)


KEY_NONE = -2 ** 31
KEY_MAX = 2 ** 31 - 1


def _order_key(x):
    b = lax.bitcast_convert_type(x, jnp.int32)
    return b ^ (jnp.right_shift(b, 31) & KEY_MAX)


def _order_key_inv(k):
    return lax.bitcast_convert_type(k ^ (jnp.right_shift(k, 31) & KEY_MAX), F32)


def _dsa_kernel(bq_ref, idx_ref, bk_ref, bv_ref, ki_ref, bias_ref, o_ref, sc_ref, sct_ref, msk_ref,
                *, tq, tqs, tkc, past, s_valid, topk):
    i = pl.program_id(1)
    q0 = past + i * tq
    qpos = q0 + lax.broadcasted_iota(jnp.int32, (tq, 1), 0)
    q_lim = _visible_limit(qpos, s_valid)
    lim_last = _visible_limit(q0 + tq - 1, s_valid)
    nch = (lim_last + tkc - 1) // tkc
    ngrp = tkc // LANES
    kf = float(topk)

    def chunk(c):
        return pl.ds(pl.multiple_of(c * tkc, tkc), tkc)

    def key_pos(c):
        return c * tkc + lax.broadcasted_iota(jnp.int32, (1, tkc), 1)

    def key_pos_col(c):
        return (c * tkc + lax.broadcasted_iota(jnp.int32, (tkc, 1), 0)).astype(F32)

    idx = idx_ref[...]
    lane = lax.broadcasted_iota(jnp.int32, (1, LANES), 1)
    qi = idx[:, :LANES]
    qhs = [jnp.where((lane >= h * IDX_DIM) & (lane < (h + 1) * IDX_DIM), qi, 0.0).astype(BF16)
           for h in range(IDX_HEADS)]
    wi = idx[:, LANES + IDX_DIM:LANES + IDX_DIM + IDX_HEADS] * IDX_SCALE
    whs = [wi[:, h:h + 1] for h in range(IDX_HEADS)]

    def score_chunk(c, carry):
        ki = ki_ref[chunk(c), :]
        tot = None
        for h in range(IDX_HEADS):
            d = lax.dot_general(qhs[h], ki, _NT, preferred_element_type=F32)
            term = whs[h] * jnp.maximum(d, 0.0)
            tot = term if tot is None else tot + term
        sc = jnp.where(key_pos(c) < q_lim, _order_key(tot), KEY_NONE)
        sc_ref[:, chunk(c)] = sc
        if tqs > tq:
            sc = jnp.concatenate([sc, jnp.full((tqs - tq, tkc), KEY_NONE, jnp.int32)], axis=0)
        for g in range(ngrp):
            rows = pl.ds(pl.multiple_of(c * tkc + g * LANES, LANES), LANES)
            for rg in range(tqs // LANES):
                sct_ref[rg, rows, :] = sc[rg * LANES:(rg + 1) * LANES, g * LANES:(g + 1) * LANES].T
        return carry

    def keys_t(c):
        return jnp.concatenate([sct_ref[rg, chunk(c), :] for rg in range(tqs // LANES)], axis=1)

    lax.fori_loop(0, nch, score_chunk, 0)

    def fold8(x, op):
        pair = {jnp.sum: jnp.add, jnp.min: jnp.minimum, jnp.max: jnp.maximum}[op]
        parts = [x[8 * r:8 * (r + 1)] for r in range(tkc // 8)]
        while len(parts) > 1:
            odd = parts[len(parts) - 1:] if len(parts) % 2 else []
            parts = [pair(parts[r], parts[r + 1]) for r in range(0, len(parts) - 1, 2)] + odd
        return parts[0]

    def count_where(pred):
        def body(c, acc):
            m = jnp.where(pred(keys_t(c), c), 1.0, 0.0)
            return acc + fold8(m, jnp.sum)
        acc = lax.fori_loop(0, nch, body, jnp.zeros((8, tqs), F32))
        return jnp.sum(acc, axis=0, keepdims=True)

    def count_gt(t):
        return count_where(lambda x, c: x > t)

    def search():
        lane_q = lax.broadcasted_iota(jnp.int32, (1, tqs), 1)
        nvis = jnp.where(lane_q < tq, _visible_limit(q0 + lane_q, s_valid), 0).astype(F32)

        def mm_body(c, carry):
            mn, mx = carry
            x = keys_t(c)
            xm = jnp.where(x == KEY_NONE, KEY_MAX, x)
            return (jnp.minimum(mn, fold8(xm, jnp.min)), jnp.maximum(mx, fold8(x, jnp.max)))
        mn, mx = lax.fori_loop(0, nch, mm_body, (jnp.full((8, tqs), KEY_MAX, jnp.int32),
                                                 jnp.full((8, tqs), KEY_NONE, jnp.int32)))
        mn = jnp.min(mn, axis=0, keepdims=True)
        mx = jnp.max(mx, axis=0, keepdims=True)
        done0 = jnp.where(nvis <= kf, 1.0, 0.0)
        mn = jnp.where(nvis > 0.5, mn, 0)
        mx = jnp.where(nvis > 0.5, mx, 0)
        lo0 = mn - 1
        zeros = jnp.zeros((1, tqs), F32)
        izeros = jnp.zeros((1, tqs), jnp.int32)

        def snap(lo, hi, clo, chi, t, done, tie, vt):
            def body(c, carry):
                a, b = carry
                x = keys_t(c)
                xa = jnp.where(x > lo, x, KEY_MAX)
                xb = jnp.where(x <= hi, x, KEY_NONE)
                return (jnp.minimum(a, fold8(xa, jnp.min)), jnp.maximum(b, fold8(xb, jnp.max)))
            a, b = lax.fori_loop(0, nch, body, (jnp.full((8, tqs), KEY_MAX, jnp.int32),
                                                jnp.full((8, tqs), KEY_NONE, jnp.int32)))
            a = jnp.min(a, axis=0, keepdims=True)
            b = jnp.max(b, axis=0, keepdims=True)
            nd = done < 0.5
            open_ = nd & (a < b)
            b1 = jnp.where(open_, b - 1, hi)
            a0 = jnp.where(open_, a, lo)
            cb = count_gt(b1)
            ca = count_gt(a0)
            b_hit = open_ & (cb == kf)
            b_tie = open_ & (cb > kf)
            rest = open_ & (cb < kf)
            a_hit = rest & (ca == kf)
            a_tie = rest & (ca < kf)
            rest = rest & (ca > kf)
            new_tie = (nd & (a >= b)) | b_tie | a_tie
            t = jnp.where(b_hit, b1, jnp.where(a_hit, a0, t))
            vt = jnp.where(a_tie, a, jnp.where(new_tie, b, vt))
            tie = jnp.where(new_tie, 1.0, tie)
            done = jnp.where(new_tie | b_hit | a_hit, 1.0, done)
            lo = jnp.where(rest, a0, lo)
            clo = jnp.where(rest, ca, clo)
            hi = jnp.where(rest, b1, hi)
            chi = jnp.where(rest, cb, chi)
            return lo, hi, clo, chi, t, done, tie, vt

        def cond(s):
            return (s[1] > 0) & (s[0] < SEARCH_MAX_IT)

        def one_pass(it, s):
            lo, hi, clo, chi, t, done, tie, vt = s
            stuck = hi <= lo + 1
            lo_f, hi_f = _order_key_inv(lo), _order_key_inv(hi)
            itp = lo_f + (hi_f - lo_f) * ((clo - kf) / (clo - chi))
            cand = _order_key(jnp.where(it % 2 == 1, itp, 0.5 * lo_f + 0.5 * hi_f))
            imid = (lo & hi) + jnp.right_shift(lo ^ hi, 1)
            mid = jnp.where((cand > lo) & (cand < hi), cand, imid)
            c = count_gt(mid)
            nd = done < 0.5
            live = nd & jnp.logical_not(stuck)
            hit = live & (c == kf)
            stk = nd & stuck
            up = live & (c > kf)
            dn = live & (c < kf)
            t = jnp.where(hit, mid, t)
            vt = jnp.where(stk, hi, vt)
            tie = jnp.where(stk, 1.0, tie)
            lo = jnp.where(up, mid, lo)
            clo = jnp.where(up, c, clo)
            hi = jnp.where(dn, mid, hi)
            chi = jnp.where(dn, c, chi)
            done = jnp.where(hit | stk, 1.0, done)
            return lo, hi, clo, chi, t, done, tie, vt

        def body(s):
            it = s[0] + 2
            st = one_pass(it - 1, one_pass(it - 2, s[2:]))
            do_snap = (it >= SNAP_START) & ((it - SNAP_START) % SNAP_EVERY == 0)
            lo, hi, clo, chi, t, done, tie, vt = lax.cond(do_snap, snap, lambda *a: a, *st)
            active = (jnp.max(1.0 - done) > 0.5).astype(jnp.int32)
            return it, active, lo, hi, clo, chi, t, done, tie, vt

        active0 = (jnp.max(1.0 - done0) > 0.5).astype(jnp.int32)
        s = lax.while_loop(cond, body, (jnp.int32(0), active0, lo0, mx, nvis, zeros, lo0, done0,
                                        zeros, izeros))
        t, tie, vt = s[6], s[8], s[9]

        def tie_cut():
            t2 = jnp.where(tie > 0.5, vt, t)
            need = kf - count_gt(t2)

            def bs_body(_, carry):
                lo_c, hi_c = carry
                mid_c = jnp.floor((lo_c + hi_c) * 0.5)
                cnt = count_where(lambda x, c: (x == vt) & (key_pos_col(c) < mid_c))
                ok = cnt >= need
                return jnp.where(ok, lo_c, mid_c), jnp.where(ok, mid_c, hi_c)

            nbits = int(s_valid).bit_length()
            _, cut = lax.fori_loop(0, nbits, bs_body,
                                   (zeros, jnp.full((1, tqs), float(1 << nbits), F32)))
            return t2, jnp.where(tie > 0.5, cut, 0.0)

        any_tie = jnp.max(tie) > 0.5
        t, cut = lax.cond(any_tie, tie_cut, lambda: (t, zeros))
        return t, tie, vt, cut

    def no_search():
        z = jnp.zeros((1, tqs), F32)
        return (jnp.full((1, tqs), KEY_NONE, jnp.int32), z, jnp.zeros((1, tqs), jnp.int32), z)

    t, tie, vt, cut = lax.cond(lim_last > topk, search, no_search)

    def to_col(v):
        groups = [jnp.broadcast_to(v[:, rg * LANES:(rg + 1) * LANES], (LANES, LANES)).T
                  for rg in range(tqs // LANES)]
        full = groups[0] if len(groups) == 1 else jnp.concatenate(groups, axis=0)
        return full[:tq]

    t, vt, cut = to_col(t), to_col(vt), to_col(cut)
    tied = to_col(tie) > 0.5
    lane_f = lax.broadcasted_iota(jnp.int32, (1, LANES), 1).astype(F32)

    def to_mask(c, carry):
        for g in range(ngrp):
            cols = pl.ds(pl.multiple_of(c * tkc + g * LANES, LANES), LANES)
            x = sc_ref[:, cols]
            kpos = (c * tkc + g * LANES).astype(F32) + lane_f
            sel = (x > t) | (tied & (x == vt) & (kpos < cut))
            msk_ref[:, cols] = jnp.where(sel, 0.0, NEG)
        return carry

    lax.fori_loop(0, nch, to_mask, 0)

    bq = (bq_ref[...] * B_SCALE).astype(BF16)
    qbs = [bq[:, h * B_DIM:(h + 1) * B_DIM] for h in range(B_HEADS)]
    d0 = q0 // LANES

    def att_body(c, carries):
        ks = chunk(c)
        msk = msk_ref[:, ks]
        rows = min(tq, LANES)
        out = []
        for h in range(B_HEADS):
            s = lax.dot_general(qbs[h], bk_ref[h, ks, :], _NT, preferred_element_type=F32)
            bias = jnp.concatenate(
                [jnp.concatenate([bias_ref[jnp.clip(d0 + rg - (c * ngrp + g), 0, 2), h, :rows, :]
                                  for g in range(ngrp)], axis=1)
                 for rg in range(-(-tq // LANES))], axis=0)
            out.append(_online(carries[h], s + (msk + bias), bv_ref[h, ks, :]))
        return tuple(out)

    init = tuple(_softmax_init(tq, B_DIM) for _ in range(B_HEADS))
    carries = lax.fori_loop(0, nch, att_body, init)
    for h in range(B_HEADS):
        _, l, acc = carries[h]
        o_ref[:, h * B_DIM:(h + 1) * B_DIM] = acc / l


def _dsa(proj3, bk, bv, ki, bias, tq, tkc, past, s_valid, topk):
    b, t, _ = proj3.shape
    s_pad = bk.shape[2]
    tqs = -(-tq // LANES) * LANES
    assert past % LANES == 0 and (tq % LANES == 0 or t == tq) and s_pad % tkc == 0
    kern = functools.partial(_dsa_kernel, tq=tq, tqs=tqs, tkc=tkc, past=past, s_valid=s_valid,
                             topk=topk)
    return pl.pallas_call(
        kern,
        out_shape=jax.ShapeDtypeStruct((b, t, 256), F32),
        grid=(b, t // tq),
        in_specs=[pl.BlockSpec((None, tq, 256), lambda bi, i: (bi, i, CB_BQ)),
                  pl.BlockSpec((None, tq, 256), lambda bi, i: (bi, i, CB_IDX)),
                  pl.BlockSpec((None, B_HEADS, s_pad, B_DIM), lambda bi, i: (bi, 0, 0, 0)),
                  pl.BlockSpec((None, B_HEADS, s_pad, B_DIM), lambda bi, i: (bi, 0, 0, 0)),
                  pl.BlockSpec((None, s_pad, LANES), lambda bi, i: (bi, 0, 0)),
                  pl.BlockSpec((3, B_HEADS, LANES, LANES), lambda bi, i: (0, 0, 0, 0))],
        out_specs=pl.BlockSpec((None, tq, 256), lambda bi, i: (bi, i, 0)),
        scratch_shapes=[pltpu.VMEM((tq, s_pad), jnp.int32), pltpu.VMEM((tqs // LANES, s_pad, LANES), jnp.int32),
                        pltpu.VMEM((tq, s_pad), F32)],
        compiler_params=_cparams(("parallel", "arbitrary")),
        name="dsa_attn",
    )(proj3, proj3, bk, bv, ki, bias)


def _pool_kernel(u_ref, prev_ref, w_ref, scale_ref, o_ref, st_ref, ext_ref, *, tr, past):
    i = pl.program_id(1)

    @pl.when(i == 0)
    def _():
        ext_ref[1:16, :] = prev_ref[...]

    u = u_ref[...]
    ext_ref[16:16 + tr, :] = u

    def tap(k, lo):
        return ext_ref[16 - k:16 - k + tr, lo:lo + LANES]

    pos1 = past + i * tr + 1 + lax.broadcasted_iota(jnp.int32, (tr, 1), 0)
    cnt = [jnp.minimum(w, pos1).astype(F32) for w in POOL_WINDOWS]
    lane = lax.broadcasted_iota(jnp.int32, (1, LANES), 1)
    first = lane < POOL_GROUP

    s2 = tap(0, 0) + tap(1, 0)
    s4 = s2 + tap(2, 0) + tap(3, 0)
    s8 = tap(0, LANES)
    for k in range(1, 8):
        s8 = s8 + tap(k, LANES)
    s16 = s8
    for k in range(8, 16):
        s16 = s16 + tap(k, LANES)
    m_lo = jnp.where(first, s2 / cnt[0], s4 / cnt[1])
    m_hi = jnp.where(first, s8 / cnt[2], s16 / cnt[3])
    pooled = jnp.concatenate([m_lo, m_hi], axis=1) - u
    mixed = jnp.dot(pooled.astype(BF16), w_ref[...], preferred_element_type=F32)
    o_ref[...] = mixed * scale_ref[...]

    new_state = ext_ref[1 + tr:16 + tr, :]
    st_ref[...] = new_state
    ext_ref[1:16, :] = new_state


def _pool(proj3, prev, w_bd, scale, tr, past):
    b, t, _ = proj3.shape
    kern = functools.partial(_pool_kernel, tr=tr, past=past)
    return pl.pallas_call(
        kern,
        out_shape=(jax.ShapeDtypeStruct((b, t, POOL_WIDTH), F32),
                   jax.ShapeDtypeStruct((b, POOL_STATE, POOL_WIDTH), F32)),
        grid=(b, t // tr),
        in_specs=[pl.BlockSpec((None, tr, 256), lambda bi, i: (bi, i, CB_POOL)),
                  pl.BlockSpec((None, POOL_STATE, POOL_WIDTH), lambda bi, i: (bi, 0, 0)),
                  pl.BlockSpec((256, 256), lambda bi, i: (0, 0)),
                  pl.BlockSpec((1, 256), lambda bi, i: (0, 0))],
        out_specs=(pl.BlockSpec((None, tr, 256), lambda bi, i: (bi, i, 0)),
                   pl.BlockSpec((None, POOL_STATE, POOL_WIDTH), lambda bi, i: (bi, 0, 0))),
        scratch_shapes=[pltpu.VMEM((tr + 16, POOL_WIDTH), F32)],
        compiler_params=_cparams(("parallel", "arbitrary")),
        name="pool_mix",
    )(proj3, prev, w_bd, scale)


def _conv_kernel(u_ref, prev_ref, w_ref, b_ref, g_ref, beta_ref, o_ref, st_ref, ext_ref, *, tr, sub):
    i = pl.program_id(1)
    npre = CONV_K - 1

    @pl.when(i == 0)
    def _():
        ext_ref[2:2 + npre, :] = prev_ref[...]

    u = u_ref[...]
    ext_ref[32:32 + tr, :] = u[:, :CONV_WIDTH] * _sigmoid(u[:, CONV_WIDTH:])

    for r0 in range(0, tr, sub):
        acc = ext_ref[r0 + 2:r0 + 2 + sub, :] * w_ref[0:1, :]
        for k in range(1, CONV_K):
            acc = acc + ext_ref[r0 + 2 + k:r0 + 2 + k + sub, :] * w_ref[k:k + 1, :]
        y = _ln_rows(acc + b_ref[...], g_ref[...], beta_ref[...])
        o_ref[r0:r0 + sub, :] = y * _sigmoid(y)

    new_state = ext_ref[2 + tr:2 + tr + npre, :]
    st_ref[...] = new_state
    ext_ref[2:2 + npre, :] = new_state


def _conv(proj3, prev, w, bias, g, beta, tr, sub):
    b, t, _ = proj3.shape
    npre = CONV_K - 1
    kern = functools.partial(_conv_kernel, tr=tr, sub=sub)
    vec = pl.BlockSpec((1, CONV_WIDTH), lambda bi, i: (0, 0))
    return pl.pallas_call(
        kern,
        out_shape=(jax.ShapeDtypeStruct((b, t, CONV_WIDTH), F32),
                   jax.ShapeDtypeStruct((b, npre, CONV_WIDTH), F32)),
        grid=(b, t // tr),
        in_specs=[pl.BlockSpec((None, tr, 512), lambda bi, i: (bi, i, CB_CONV)),
                  pl.BlockSpec((None, npre, CONV_WIDTH), lambda bi, i: (bi, 0, 0)),
                  pl.BlockSpec((CONV_K, CONV_WIDTH), lambda bi, i: (0, 0)),
                  vec, vec, vec],
        out_specs=(pl.BlockSpec((None, tr, CONV_WIDTH), lambda bi, i: (bi, i, 0)),
                   pl.BlockSpec((None, npre, CONV_WIDTH), lambda bi, i: (bi, 0, 0))),
        scratch_shapes=[pltpu.VMEM((tr + 32, CONV_WIDTH), F32)],
        compiler_params=_cparams(("parallel", "arbitrary")),
        name="conv_module",
    )(proj3, prev, w, bias, g, beta)


def _merge_kernel(a_ref, b_ref, c_ref, d_ref, x_ref, wg_ref, wb_ref, wo_ref, g_ref, beta_ref,
                  o_ref):
    x = x_ref[...]
    xb = x.astype(BF16)
    mixed = None
    for n, br_ref in enumerate((a_ref, b_ref, c_ref, d_ref)):
        gate = jnp.dot(xb, wg_ref[:, n * D_MODEL:(n + 1) * D_MODEL], preferred_element_type=F32)
        br = jnp.dot(br_ref[...].astype(BF16), wb_ref[n], preferred_element_type=F32)
        term = _sigmoid(gate) * br
        mixed = term if mixed is None else mixed + term
    y = jnp.dot(mixed.astype(BF16), wo_ref[...], preferred_element_type=F32)
    o_ref[...] = _ln_rows(ALPHA * x + y, g_ref[...], beta_ref[...])


def _merge(oa, ob, oc, od, x3, wg, wb, wo, g, beta, tm):
    b, t, _ = x3.shape
    br_spec = pl.BlockSpec((None, tm, 256), lambda bi, i: (bi, i, 0))
    const = lambda shape: pl.BlockSpec(shape, lambda bi, i: (0,) * len(shape),
                                       pipeline_mode=pl.Buffered(1))
    return pl.pallas_call(
        _merge_kernel,
        out_shape=jax.ShapeDtypeStruct((b, t, D_MODEL), F32),
        grid=(b, t // tm),
        in_specs=[br_spec, br_spec, br_spec, br_spec,
                  pl.BlockSpec((None, tm, D_MODEL), lambda bi, i: (bi, i, 0)),
                  const((D_MODEL, N_BRANCH * D_MODEL)), const((N_BRANCH, 256, D_MODEL)),
                  const((D_MODEL, D_MODEL)), const((1, D_MODEL)), const((1, D_MODEL))],
        out_specs=pl.BlockSpec((None, tm, D_MODEL), lambda bi, i: (bi, i, 0)),
        compiler_params=_cparams(("parallel", "parallel")),
        name="merge_ln1",
    )(oa, ob, oc, od, x3, wg, wb, wo, g, beta)


FF_TC = 256
FF_NC = D_FF // FF_TC


def _ffn_kernel(x_ref, st_ref, wup_ref, cw_ref, cb_ref, wd_ref, g_ref, beta_ref, o_ref, nst_ref,
                car_ref, ubuf_ref, h_ref, *, tm):
    @pl.when(pl.program_id(1) == 0)
    def _():
        car_ref[6:8, :] = st_ref[...]

    x = x_ref[...]
    xb = x.astype(BF16)

    def conv3(off, slot):
        cols = slice(off, off + FF_TC)
        u = jnp.dot(xb, wup_ref[:, cols], preferred_element_type=F32)
        ubuf_ref[slot, 6:8, :] = car_ref[6:8, cols]
        ubuf_ref[slot, 8:8 + tm, :] = u
        y = (cw_ref[0:1, cols] * ubuf_ref[slot, 6:6 + tm, :]
             + cw_ref[1:2, cols] * ubuf_ref[slot, 7:7 + tm, :]
             + cw_ref[2:3, cols] * u + cb_ref[:, cols])
        car_ref[6:8, cols] = ubuf_ref[slot, 6 + tm:8 + tm, :]
        return y

    for c in range(FF_NC):
        val = conv3(c * FF_TC, (2 * c) % 4)
        gate = conv3(D_FF + c * FF_TC, (2 * c + 1) % 4)
        h_ref[:, c * FF_TC:(c + 1) * FF_TC] = ((gate * _sigmoid(gate)) * val).astype(BF16)

    nst_ref[...] = car_ref[6:8, :]
    y = jnp.dot(h_ref[...], wd_ref[...], preferred_element_type=F32)
    o_ref[...] = _ln_rows(ALPHA * x + y, g_ref[...], beta_ref[...])


def _ffn(x3, st, w_up, cw, cb, w_dn, g, beta, tm):
    b, t, _ = x3.shape
    kern = functools.partial(_ffn_kernel, tm=tm)
    nk = FFN_K - 1
    const = lambda shape: pl.BlockSpec(shape, lambda bi, i: (0,) * len(shape),
                                       pipeline_mode=pl.Buffered(1))
    return pl.pallas_call(
        kern,
        out_shape=(jax.ShapeDtypeStruct((b, t, D_MODEL), F32),
                   jax.ShapeDtypeStruct((b, nk, 2 * D_FF), F32)),
        grid=(b, t // tm),
        in_specs=[pl.BlockSpec((None, tm, D_MODEL), lambda bi, i: (bi, i, 0)),
                  pl.BlockSpec((None, nk, 2 * D_FF), lambda bi, i: (bi, 0, 0)),
                  const((D_MODEL, 2 * D_FF)), const((FFN_K, 2 * D_FF)), const((1, 2 * D_FF)),
                  const((D_FF, D_MODEL)), const((1, D_MODEL)), const((1, D_MODEL))],
        out_specs=(pl.BlockSpec((None, tm, D_MODEL), lambda bi, i: (bi, i, 0)),
                   pl.BlockSpec((None, nk, 2 * D_FF), lambda bi, i: (bi, 0, 0))),
        scratch_shapes=[pltpu.VMEM((8, 2 * D_FF), F32),
                        pltpu.VMEM((4, tm + 8, FF_TC), F32),
                        pltpu.VMEM((tm, D_FF), BF16)],
        compiler_params=_cparams(("parallel", "arbitrary")),
        name="conv_ffn_ln2",
    )(x3, st, w_up, cw, cb, w_dn, g, beta)


def _rope_tables(past, t):
    half = A_ROPE // 2
    freqs = ROPE_BASE ** (-jnp.arange(half, dtype=F32) / half)
    pos = past + jnp.arange(t, dtype=jnp.int32)
    ang = pos.astype(F32)[:, None] * freqs[None, :]
    cos, sin = jnp.cos(ang), jnp.sin(ang)
    c2 = jnp.concatenate([cos, cos], axis=1)
    s2 = jnp.concatenate([sin, sin], axis=1)
    one = jnp.ones((t, A_NOPE), F32)
    z = lambda n: jnp.zeros((t, n), F32)
    cosq = jnp.concatenate([one, c2, z(LANES - A_NOPE - A_ROPE)], axis=1)
    sinq = jnp.concatenate([z(A_NOPE), s2, z(LANES - A_NOPE - A_ROPE)], axis=1)
    cosk = jnp.concatenate([c2, z(LANES - A_ROPE)], axis=1)
    sink = jnp.concatenate([s2, z(LANES - A_ROPE)], axis=1)
    return cosq, sinq, cosk, sink


_BUCKET_EDGES = (12, 16, 23, 32, 46, 64, 91)


def _rel_bucket(rel):
    nb = REL_BUCKETS // 2
    max_exact = nb // 2
    n = np.abs(rel)
    large = max_exact + sum((n >= e).astype(np.int64) for e in _BUCKET_EDGES)
    return np.where(rel > 0, nb, 0) + np.where(n < max_exact, n, large)


def _bias_tiles(rel_bias):
    r = np.arange(LANES)[:, None]
    c = np.arange(LANES)[None, :]
    buckets = np.stack([_rel_bucket(c - r - LANES * d) for d in range(3)])
    tiles = jnp.zeros((B_HEADS,) + buckets.shape, F32)
    for bkt in np.unique(buckets):
        sel = jnp.asarray(buckets == bkt)
        tiles = jnp.where(sel[None], rel_bias[int(bkt)][:, None, None, None], tiles)
    return jnp.transpose(tiles, (1, 0, 2, 3))


def _pack_w_in(w):
    z = lambda n: jnp.zeros((D_MODEL, n), F32)
    kr = w[:, O_KROPE:O_KROPE + A_ROPE]
    half = A_ROPE // 2
    kr_rot = jnp.concatenate([-kr[:, half:], kr[:, :half]], axis=1)
    parts = [
        w[:, O_CONV:O_CONV + 2 * CONV_WIDTH],
        w[:, O_CQ:O_CQ + A_Q_LORA], z(256 - A_Q_LORA),
        w[:, O_CKV:O_CKV + A_KV_LORA], kr, kr_rot, z(256 - A_KV_LORA - 2 * A_ROPE),
        w[:, O_BQ:O_BQ + 256], w[:, O_BK:O_BK + 256], w[:, O_BV:O_BV + 256],
        w[:, O_QIDX:O_QIDX + 128], w[:, O_KIDX:O_KIDX + IDX_DIM], w[:, O_WIDX:O_WIDX + IDX_HEADS],
        z(256 - 128 - IDX_DIM - IDX_HEADS),
        w[:, O_POOL:O_POOL + POOL_WIDTH],
    ]
    return jnp.concatenate(parts, axis=1).astype(BF16)


def _pack_w_qup(w):
    zrow = lambda m: jnp.zeros((256 - A_Q_LORA, m.shape[1]), F32)
    half = A_ROPE // 2
    dq = A_NOPE + A_ROPE
    cols, rcols = [], []
    for h in range(A_HEADS):
        nope = w[:, h * dq:h * dq + A_NOPE]
        rp = w[:, h * dq + A_NOPE:(h + 1) * dq]
        pad = jnp.zeros((A_Q_LORA, LANES - dq), F32)
        cols += [nope, rp, pad]
        rcols += [jnp.zeros_like(nope), -rp[:, half:], rp[:, :half], pad]
    wq = jnp.concatenate(cols, axis=1)
    wqr = jnp.concatenate(rcols, axis=1)
    wq = jnp.concatenate([wq, zrow(wq)], axis=0).astype(BF16)
    wqr = jnp.concatenate([wqr, zrow(wqr)], axis=0).astype(BF16)
    return wq, wqr


def _pack_w_kvup(w):
    dk = A_NOPE + A_V
    kc, vc = [], []
    for h in range(A_HEADS):
        kc += [w[:, h * dk:h * dk + A_NOPE], jnp.zeros((A_KV_LORA, LANES - A_NOPE), F32)]
        vc += [w[:, h * dk + A_NOPE:(h + 1) * dk]]
    return jnp.concatenate(kc, axis=1).astype(BF16), jnp.concatenate(vc, axis=1).astype(BF16)


def _rope_placement():
    e = np.zeros((LANES, A_HEADS * LANES), np.float32)
    for h in range(A_HEADS):
        for r in range(A_ROPE):
            e[r, h * LANES + A_NOPE + r] = 1.0
    return jnp.asarray(e, dtype=BF16)


def _pool_blockdiag(pw):
    out = jnp.zeros((POOL_WIDTH, POOL_WIDTH), F32)
    for g in range(len(POOL_WINDOWS)):
        out = out.at[g * POOL_GROUP:(g + 1) * POOL_GROUP, g * POOL_GROUP:(g + 1) * POOL_GROUP].set(pw[g])
    return out.astype(BF16)


class _Tiles:
    def __init__(self, t):
        prompt = t > 128
        self.rows = 1024 if prompt else 128
        self.in_tn = 1152
        self.prep = 512 if prompt else t
        self.kv = 512
        self.mla_q = 512 if prompt else t
        self.mla_k = 1024 if prompt else 17 * LANES
        self.dsa_q = 256 if prompt else t
        self.dsa_kc = 512 if prompt else 17 * LANES
        self.pool = 512 if prompt else t
        self.conv = 256 if prompt else t
        self.conv_sub = 64 if prompt else t
        self.merge = 512 if prompt else t
        self.ffn = 512 if prompt else t
        self.s_align = 512 if prompt else 17 * LANES


def _trunk_layer(x3, caches, lw, consts, past, stacks, layer):
    (w_in, w_gates, gq, gkv, wq, wqr, wkn, wv, pool_bd, pool_scale, conv_w, conv_b, conv_g, conv_beta,
     w_branch, w_out, ln1_g, ln1_b, w_up, ffn_cw, ffn_cb, w_down, ln2_g, ln2_b) = lw
    tabs, bias, e_place = consts
    b, t, _ = x3.shape
    tl = _Tiles(t)
    s_valid = past + t
    s_pad = -(-s_valid // tl.s_align) * tl.s_align
    topk = min(TOPK_MAX, s_valid // 4)

    proj = _inproj(x3.reshape(b * t, D_MODEL), w_in, tl.rows, tl.in_tn)
    proj3 = proj.reshape(b, t, PROJ_W)
    (qa, kr128, bk_hm, bv_hm, ki_rep), stacks = _aprep(
        proj3, tabs, gq, gkv, wq, wqr, stacks, layer, tl.prep)
    ckv_n = stacks[0][layer]

    def keys(cache, new, width):
        parts = [new] if cache is None else [cache.reshape(b, past, -1), new]
        if s_pad > s_valid:
            parts.append(jnp.zeros((b, s_pad - s_valid, width), new.dtype))
        return parts[0] if len(parts) == 1 else jnp.concatenate(parts, axis=1)

    c_ckv, c_krope, c_bk, c_bv, c_kidx, st_pool, st_conv, st_ffn = caches
    ckv_all = keys(c_ckv, ckv_n, A_KV_LORA)
    if c_krope is None:
        kr_all = keys(None, kr128, LANES)
    else:
        c_kr128 = jnp.pad(c_krope, ((0, 0), (0, 0), (0, LANES - A_ROPE)))
        kr_all = keys(c_kr128, kr128, LANES)
    if c_bk is None and s_pad == s_valid:
        bk_all, bv_all, ki_all = bk_hm, bv_hm, ki_rep
    else:
        head_major = lambda a: jnp.transpose(a.astype(BF16).reshape(b, s_pad, B_HEADS, B_DIM),
                                             (0, 2, 1, 3))
        bk_all = head_major(keys(c_bk, stacks[2][layer], 256))
        bv_all = head_major(keys(c_bv, stacks[3][layer], 256))
        ki_all = jnp.tile(keys(c_kidx, stacks[4][layer], IDX_DIM).astype(BF16), (1, 1, IDX_HEADS))

    kx, vx = _kvup(ckv_all.reshape(b * s_pad, A_KV_LORA), kr_all.reshape(b * s_pad, LANES),
                   wkn, wv, e_place, tl.kv)
    out_a = _mla(qa, kx.reshape(b, s_pad, 512), vx.reshape(A_HEADS, b, s_pad, A_V),
                 tl.mla_q, tl.mla_k, past, s_valid)
    out_b = _dsa(proj3, bk_all, bv_all, ki_all, bias, tl.dsa_q, tl.dsa_kc, past, s_valid, topk)
    out_c, new_pool = _pool(proj3, st_pool, pool_bd, pool_scale, tl.pool, past)
    out_d, new_conv = _conv(proj3, st_conv, conv_w, conv_b, conv_g, conv_beta, tl.conv, tl.conv_sub)
    x1 = _merge(out_a, out_b, out_c, out_d, x3, w_gates, w_branch, w_out, ln1_g, ln1_b, tl.merge)
    x2, new_ffn = _ffn(x1, st_ffn, w_up, ffn_cw, ffn_cb, w_down, ln2_g, ln2_b, tl.ffn)
    return x2, stacks, (new_pool, new_conv, new_ffn)


def kernel(x_prompt, x_sample, cache_a_ckv, cache_a_krope, cache_b_k, cache_b_v, cache_b_kidx,
           state_pool, state_conv, state_ffn, rel_bias, ln_in_g, ln_in_b, w_in, a_q_norm, a_kv_norm,
           a_w_qup, a_w_kvup, pool_w, pool_scale, conv_w, conv_b, conv_ln_g, conv_ln_b, w_branch,
           w_out, ln1_g, ln1_b, w_up, ffn_conv_w, ffn_conv_b, w_down, ln2_g, ln2_b):
    bp, tp, _ = x_prompt.shape
    bs, ts, _ = x_sample.shape
    past = cache_a_ckv.shape[2]
    depth = w_in.shape[0]

    xp = _layer_norm(x_prompt.reshape(bp * tp, D_MODEL), ln_in_g, ln_in_b,
                     _Tiles(tp).rows).reshape(bp, tp, D_MODEL)
    xs = _layer_norm(x_sample.reshape(bs * ts, D_MODEL), ln_in_g, ln_in_b,
                     _Tiles(ts).rows).reshape(bs, ts, D_MODEL)

    e_place = _rope_placement()
    bias = _bias_tiles(rel_bias)
    consts_p = (_rope_tables(0, tp), bias, e_place)
    consts_s = (_rope_tables(past, ts), bias, e_place)
    zeros_p = (None, None, None, None, None,
               jnp.zeros((bp, POOL_STATE, POOL_WIDTH), F32),
               jnp.zeros((bp, CONV_K - 1, CONV_WIDTH), F32),
               jnp.zeros((bp, FFN_K - 1, 2 * D_FF), F32))

    def state_stacks(b, t):
        return tuple(jnp.zeros((depth, b, t, w), F32)
                     for w in (A_KV_LORA, A_ROPE, 256, 256, IDX_DIM))

    stacks_p, stacks_s = state_stacks(bp, tp), state_stacks(bs, ts)
    row = lambda v: v.reshape(1, -1)
    p_states, s_states = [], []
    for l in range(depth):
        wq, wqr = _pack_w_qup(a_w_qup[l])
        wkn, wv = _pack_w_kvup(a_w_kvup[l])
        gq = jnp.concatenate([a_q_norm[l], jnp.zeros((256 - A_Q_LORA,), F32)]).reshape(1, 256)
        w_gates = w_in[l][:, O_GATES:O_GATES + N_BRANCH * D_MODEL].astype(BF16)
        lw = (_pack_w_in(w_in[l]), w_gates, gq, row(a_kv_norm[l]), wq, wqr, wkn, wv,
              _pool_blockdiag(pool_w[l]), row(pool_scale[l]), conv_w[l], row(conv_b[l]),
              row(conv_ln_g[l]), row(conv_ln_b[l]), w_branch[l].astype(BF16), w_out[l].astype(BF16),
              row(ln1_g[l]), row(ln1_b[l]), w_up[l].astype(BF16), ffn_conv_w[l], row(ffn_conv_b[l]),
              w_down[l].astype(BF16), row(ln2_g[l]), row(ln2_b[l]))
        xp, stacks_p, st_p = _trunk_layer(xp, zeros_p, lw, consts_p, 0, stacks_p, l)
        caches_s = (cache_a_ckv[l], cache_a_krope[l], cache_b_k[l], cache_b_v[l], cache_b_kidx[l],
                    state_pool[l], state_conv[l], state_ffn[l])
        xs, stacks_s, st_s = _trunk_layer(xs, caches_s, lw, consts_s, past, stacks_s, l)
        p_states.append(st_p)
        s_states.append(st_s)

    def outputs(stacks, small, b, t):
        ckv, krope, bk, bv, kidx = stacks
        heads = lambda a: a.reshape(depth, b, t, B_HEADS, B_DIM)
        return [ckv, krope, heads(bk), heads(bv), kidx] + [jnp.stack(a) for a in zip(*small)]

    return (xp, xs, *outputs(stacks_p, p_states, bp, tp), *outputs(stacks_s, s_states, bs, ts))
```

```python
import functools

import numpy as np
import jax
import jax.numpy as jnp
from jax import lax
from jax.experimental import pallas as pl
from jax.experimental.pallas import tpu as pltpu

F32 = jnp.float32
BF16 = jnp.bfloat16

D_MODEL = 1024
DEPTH = 4
CHUNK = 64
CHUNK_SHIFT = 6
N_BRANCH = 4
A_HEADS = 4
A_NOPE = 64
A_ROPE = 32
A_V = 64
A_Q_LORA = 192
A_KV_LORA = 128
A_SCALE = (A_NOPE + A_ROPE) ** -0.5
ROPE_BASE = 10000.0
B_HEADS = 4
B_DIM = 64
B_SCALE = B_DIM ** -0.5
IDX_HEADS = 4
IDX_DIM = 32
IDX_SCALE = (IDX_HEADS ** -0.5) * (IDX_DIM ** -0.5)
TOPK_MAX = 256
REL_BUCKETS = 32
POOL_WINDOWS = (2, 4, 8, 16)
POOL_WIDTH = 256
POOL_GROUP = 64
POOL_STATE = 15
CONV_WIDTH = 256
CONV_K = 31
D_FF = 2816
FFN_K = 3
ALPHA = (2 * DEPTH) ** 0.25
LN_EPS = 1e-5

IN_SPLITS = (A_Q_LORA, A_KV_LORA, A_ROPE, 256, 256, 256, 128, IDX_DIM, IDX_HEADS,
             POOL_WIDTH, 2 * CONV_WIDTH, N_BRANCH * D_MODEL)
_OFF = [0] + [int(v) for v in np.cumsum(IN_SPLITS)]
(O_CQ, O_CKV, O_KROPE, O_BQ, O_BK, O_BV, O_QIDX, O_KIDX, O_WIDX, O_POOL, O_CONV, O_GATES,
 D_IN) = _OFF

PROJ_W = 2304
CB_CONV = 0
CB_CQ = 2
CB_CK = 3
CB_BQ = 4
CB_BK = 5
CB_BV = 6
CB_IDX = 7
CB_POOL = 8

VMEM_LIMIT_BYTES = 48 * 1024 * 1024
LANES = 128

LOG2E = 1.4426950408889634
NEG = -1e30
M_INIT = -5e29
SEARCH_MAX_IT = 320
SNAP_START = 14
SNAP_EVERY = 4

_NT = (((1,), (1,)), ((), ()))


def _cparams(sem):
    return pltpu.CompilerParams(dimension_semantics=sem, vmem_limit_bytes=VMEM_LIMIT_BYTES)


def _ln_rows(x, g, b):
    mu = jnp.mean(x, axis=-1, keepdims=True)
    xc = x - mu
    var = jnp.mean(xc * xc, axis=-1, keepdims=True)
    return xc * lax.rsqrt(var + LN_EPS) * g + b


def _sigmoid(x):
    return 1.0 / (1.0 + jnp.exp(-x))


def _ln_kernel(x_ref, g_ref, b_ref, o_ref):
    o_ref[...] = _ln_rows(x_ref[...], g_ref[...], b_ref[...])


def _layer_norm(x2d, g, b, tm):
    n, d = x2d.shape
    return pl.pallas_call(
        _ln_kernel,
        out_shape=jax.ShapeDtypeStruct((n, d), F32),
        grid=(n // tm,),
        in_specs=[pl.BlockSpec((tm, d), lambda i: (i, 0)),
                  pl.BlockSpec((1, d), lambda i: (0, 0)),
                  pl.BlockSpec((1, d), lambda i: (0, 0))],
        out_specs=pl.BlockSpec((tm, d), lambda i: (i, 0)),
        compiler_params=_cparams(("parallel",)),
        name="input_ln",
    )(x2d, g.reshape(1, d), b.reshape(1, d))


def _inproj_kernel(x_ref, w_ref, o_ref, xb_ref):
    @pl.when(pl.program_id(1) == 0)
    def _():
        xb_ref[...] = x_ref[...].astype(BF16)

    o_ref[...] = jnp.dot(xb_ref[...], w_ref[...], preferred_element_type=F32)


def _inproj(x2d, w, tm, tn):
    n, k = x2d.shape
    nw = w.shape[1]
    return pl.pallas_call(
        _inproj_kernel,
        out_shape=jax.ShapeDtypeStruct((n, nw), F32),
        grid=(n // tm, nw // tn),
        in_specs=[pl.BlockSpec((tm, k), lambda i, j: (i, 0)),
                  pl.BlockSpec((k, tn), lambda i, j: (0, j))],
        out_specs=pl.BlockSpec((tm, tn), lambda i, j: (i, j)),
        scratch_shapes=[pltpu.VMEM((tm, k), BF16)],
        compiler_params=_cparams(("parallel", "arbitrary")),
        name="in_proj",
    )(x2d, w)


def _aprep_kernel(cq_ref, ck_ref, bk_ref, bv_ref, idx_ref, cosq_ref, sinq_ref, cosk_ref, sink_ref,
                  gq_ref, gkv_ref, wq_ref, wqr_ref, _ckv_in, _kro_in, _sbk_in, _sbv_in, _ski_in,
                  qa_ref, kr_ref, hbk_ref, hbv_ref, rki_ref,
                  ckv_ref, kro_ref, sbk_ref, sbv_ref, ski_ref):
    bk = bk_ref[...]
    bv = bv_ref[...]
    sbk_ref[...] = bk
    sbv_ref[...] = bv
    bkb = bk.astype(BF16)
    bvb = bv.astype(BF16)
    for h in range(B_HEADS):
        hbk_ref[h] = bkb[:, h * B_DIM:(h + 1) * B_DIM]
        hbv_ref[h] = bvb[:, h * B_DIM:(h + 1) * B_DIM]
    ki = idx_ref[:, LANES:]
    ski_ref[...] = ki[:, :IDX_DIM]
    lane = lax.broadcasted_iota(jnp.int32, (1, LANES), 1)
    k0 = jnp.where(lane < IDX_DIM, ki, 0.0)
    rep = k0
    for h in range(1, IDX_HEADS):
        rep = rep + pltpu.roll(k0, h * IDX_DIM, 1)
    rki_ref[...] = rep.astype(BF16)

    cq = cq_ref[...]
    ms = jnp.sum(cq * cq, axis=-1, keepdims=True) * (1.0 / A_Q_LORA)
    cqb = ((cq * lax.rsqrt(ms + LN_EPS)) * gq_ref[...]).astype(BF16)
    q = jnp.dot(cqb, wq_ref[...], preferred_element_type=F32)
    qr = jnp.dot(cqb, wqr_ref[...], preferred_element_type=F32)
    cos = jnp.concatenate([cosq_ref[...]] * A_HEADS, axis=1)
    sin = jnp.concatenate([sinq_ref[...]] * A_HEADS, axis=1)
    qa_ref[...] = ((q * cos + qr * sin) * (A_SCALE * LOG2E)).astype(BF16)

    ck = ck_ref[...]
    ckv = ck[:, :A_KV_LORA]
    ms2 = jnp.mean(ckv * ckv, axis=-1, keepdims=True)
    ckv_ref[...] = (ckv * lax.rsqrt(ms2 + LN_EPS)) * gkv_ref[...]
    kr = ck[:, A_KV_LORA:]
    krr = pltpu.roll(kr, LANES - A_ROPE, 1)
    kr_new = kr * cosk_ref[...] + krr * sink_ref[...]
    kr_ref[...] = kr_new
    kro_ref[...] = kr_new[:, :A_ROPE]


def _aprep(proj3, tabs, gq, gkv, wq, wqr, stacks, layer, tm):
    b, t, _ = proj3.shape
    cosq, sinq, cosk, sink = tabs
    tab_spec = pl.BlockSpec((tm, LANES), lambda bi, i: (i, 0))
    full = lambda shape: pl.BlockSpec(shape, lambda bi, i: (0,) * len(shape))
    blk = lambda cb: pl.BlockSpec((None, tm, 256), lambda bi, i: (bi, i, cb))
    rows = lambda w: pl.BlockSpec((None, tm, w), lambda bi, i: (bi, i, 0))
    heads = pl.BlockSpec((None, B_HEADS, tm, B_DIM), lambda bi, i: (bi, 0, i, 0))
    stacked = lambda a: pl.BlockSpec((None, None, tm, a.shape[-1]), lambda bi, i: (layer, bi, i, 0))
    untouched = pl.BlockSpec(memory_space=pl.ANY)
    n_in = 13
    out = pl.pallas_call(
        _aprep_kernel,
        out_shape=(jax.ShapeDtypeStruct((b, t, 512), BF16),
                   jax.ShapeDtypeStruct((b, t, LANES), F32),
                   jax.ShapeDtypeStruct((b, B_HEADS, t, B_DIM), BF16),
                   jax.ShapeDtypeStruct((b, B_HEADS, t, B_DIM), BF16),
                   jax.ShapeDtypeStruct((b, t, LANES), BF16))
        + tuple(jax.ShapeDtypeStruct(a.shape, a.dtype) for a in stacks),
        grid=(b, t // tm),
        in_specs=[blk(CB_CQ), blk(CB_CK), blk(CB_BK), blk(CB_BV), blk(CB_IDX),
                  tab_spec, tab_spec, tab_spec, tab_spec,
                  full((1, 256)), full((1, A_KV_LORA)), full((256, 512)), full((256, 512))]
        + [untouched] * len(stacks),
        out_specs=(rows(512), rows(LANES), heads, heads, rows(LANES))
        + tuple(stacked(a) for a in stacks),
        input_output_aliases={n_in + k: 5 + k for k in range(len(stacks))},
        compiler_params=_cparams(("parallel", "parallel")),
        name="mla_prep",
    )(proj3, proj3, proj3, proj3, proj3, cosq, sinq, cosk, sink, gq, gkv, wq, wqr, *stacks)
    return out[:5], tuple(out[5:])


def _kvup_kernel(ckv_ref, kr_ref, wkn_ref, wv_ref, e_ref, k_ref, v_ref):
    cb = ckv_ref[...].astype(BF16)
    k = jnp.dot(cb, wkn_ref[...], preferred_element_type=F32)
    k = k + jnp.dot(kr_ref[...].astype(BF16), e_ref[...], preferred_element_type=F32)
    k_ref[...] = k.astype(BF16)
    v = jnp.dot(cb, wv_ref[...], preferred_element_type=F32).astype(BF16)
    for h in range(A_HEADS):
        v_ref[h] = v[:, h * A_V:(h + 1) * A_V]


def _kvup(ckv2d, kr2d, wkn, wv, e, tm):
    m = ckv2d.shape[0]
    full = lambda shape: pl.BlockSpec(shape, lambda i: (0,) * len(shape))
    return pl.pallas_call(
        _kvup_kernel,
        out_shape=(jax.ShapeDtypeStruct((m, 512), BF16),
                   jax.ShapeDtypeStruct((A_HEADS, m, A_V), BF16)),
        grid=(m // tm,),
        in_specs=[pl.BlockSpec((tm, A_KV_LORA), lambda i: (i, 0)),
                  pl.BlockSpec((tm, LANES), lambda i: (i, 0)),
                  full((A_KV_LORA, 512)), full((A_KV_LORA, 256)), full((LANES, 512))],
        out_specs=(pl.BlockSpec((tm, 512), lambda i: (i, 0)),
                   pl.BlockSpec((A_HEADS, tm, A_V), lambda i: (0, i, 0))),
        compiler_params=_cparams(("parallel",)),
        name="kv_up",
    )(ckv2d, kr2d, wkn, wv, e)


def _online(carry, s, v):
    m, l, acc = carry
    m_new = jnp.maximum(m, jnp.max(s, axis=1, keepdims=True))
    alpha = jnp.exp2(m - m_new)
    p = jnp.exp2(s - m_new)
    l = alpha * l + jnp.sum(p, axis=1, keepdims=True)
    acc = alpha * acc + jnp.dot(p.astype(BF16), v, preferred_element_type=F32)
    return m_new, l, acc


def _softmax_init(tq, dv):
    return (jnp.full((tq, 1), M_INIT, F32), jnp.zeros((tq, 1), F32), jnp.zeros((tq, dv), F32))


def _visible_limit(qpos, s_valid):
    return jnp.minimum((jnp.right_shift(qpos, CHUNK_SHIFT) + 1) * CHUNK, s_valid)


def _mla_kernel(q_ref, k_ref, v_ref, o_ref, *, tq, tk, past, s_valid):
    i = pl.program_id(1)
    q0 = past + i * tq
    qpos = q0 + lax.broadcasted_iota(jnp.int32, (tq, 1), 0)
    q_lim = _visible_limit(qpos, s_valid)
    lim_first = _visible_limit(q0, s_valid)
    lim_last = _visible_limit(q0 + tq - 1, s_valid)
    n_full = lim_first // tk
    n_tot = (lim_last + tk - 1) // tk
    q = q_ref[...]
    qhs = [q[:, h * LANES:(h + 1) * LANES] for h in range(A_HEADS)]

    def body(j, carries, masked):
        ks = pl.ds(pl.multiple_of(j * tk, tk), tk)
        if masked:
            kpos = j * tk + lax.broadcasted_iota(jnp.int32, (1, tk), 1)
            vis = kpos < q_lim
        out = []
        for h in range(A_HEADS):
            s = lax.dot_general(qhs[h], k_ref[ks, h * LANES:(h + 1) * LANES], _NT,
                                preferred_element_type=F32)
            if masked:
                s = jnp.where(vis, s, NEG)
            out.append(_online(carries[h], s, v_ref[h, ks, :]))
        return tuple(out)

    init = tuple(_softmax_init(tq, A_V) for _ in range(A_HEADS))
    carries = lax.fori_loop(0, n_full, functools.partial(body, masked=False), init)
    carries = lax.fori_loop(n_full, n_tot, functools.partial(body, masked=True), carries)
    for h in range(A_HEADS):
        _, l, acc = carries[h]
        o_ref[:, h * A_V:(h + 1) * A_V] = acc / l


def _mla(qa, kx, v, tq, tk, past, s_valid):
    b, t, _ = qa.shape
    s_pad = kx.shape[1]
    kern = functools.partial(_mla_kernel, tq=tq, tk=tk, past=past, s_valid=s_valid)
    return pl.pallas_call(
        kern,
        out_shape=jax.ShapeDtypeStruct((b, t, 256), F32),
        grid=(b, t // tq),
        in_specs=[pl.BlockSpec((None, tq, 512), lambda bi, i: (bi, i, 0)),
                  pl.BlockSpec((None, s_pad, 512), lambda bi, i: (bi, 0, 0)),
                  pl.BlockSpec((A_HEADS, None, s_pad, A_V), lambda bi, i: (0, bi, 0, 0))],
        out_specs=pl.BlockSpec((None, tq, 256), lambda bi, i: (bi, i, 0)),
        compiler_params=_cparams(("parallel", "arbitrary")),
        name="mla_attn",
    )(qa, kx, v)


KEY_NONE = -2 ** 31
KEY_MAX = 2 ** 31 - 1


def _order_key(x):
    b = lax.bitcast_convert_type(x, jnp.int32)
    return b ^ (jnp.right_shift(b, 31) & KEY_MAX)


def _order_key_inv(k):
    return lax.bitcast_convert_type(k ^ (jnp.right_shift(k, 31) & KEY_MAX), F32)


def _dsa_kernel(bq_ref, idx_ref, bk_ref, bv_ref, ki_ref, bias_ref, o_ref, sc_ref, sct_ref, msk_ref,
                *, tq, tqs, tkc, past, s_valid, topk):
    i = pl.program_id(1)
    q0 = past + i * tq
    qpos = q0 + lax.broadcasted_iota(jnp.int32, (tq, 1), 0)
    q_lim = _visible_limit(qpos, s_valid)
    lim_last = _visible_limit(q0 + tq - 1, s_valid)
    nch = (lim_last + tkc - 1) // tkc
    ngrp = tkc // LANES
    kf = float(topk)

    def chunk(c):
        return pl.ds(pl.multiple_of(c * tkc, tkc), tkc)

    def key_pos(c):
        return c * tkc + lax.broadcasted_iota(jnp.int32, (1, tkc), 1)

    def key_pos_col(c):
        return (c * tkc + lax.broadcasted_iota(jnp.int32, (tkc, 1), 0)).astype(F32)

    idx = idx_ref[...]
    lane = lax.broadcasted_iota(jnp.int32, (1, LANES), 1)
    qi = idx[:, :LANES]
    qhs = [jnp.where((lane >= h * IDX_DIM) & (lane < (h + 1) * IDX_DIM), qi, 0.0).astype(BF16)
           for h in range(IDX_HEADS)]
    wi = idx[:, LANES + IDX_DIM:LANES + IDX_DIM + IDX_HEADS] * IDX_SCALE
    whs = [wi[:, h:h + 1] for h in range(IDX_HEADS)]

    def score_chunk(c, carry):
        ki = ki_ref[chunk(c), :]
        tot = None
        for h in range(IDX_HEADS):
            d = lax.dot_general(qhs[h], ki, _NT, preferred_element_type=F32)
            term = whs[h] * jnp.maximum(d, 0.0)
            tot = term if tot is None else tot + term
        sc = jnp.where(key_pos(c) < q_lim, _order_key(tot), KEY_NONE)
        sc_ref[:, chunk(c)] = sc
        if tqs > tq:
            sc = jnp.concatenate([sc, jnp.full((tqs - tq, tkc), KEY_NONE, jnp.int32)], axis=0)
        for g in range(ngrp):
            rows = pl.ds(pl.multiple_of(c * tkc + g * LANES, LANES), LANES)
            for rg in range(tqs // LANES):
                sct_ref[rg, rows, :] = sc[rg * LANES:(rg + 1) * LANES, g * LANES:(g + 1) * LANES].T
        return carry

    def keys_t(c):
        return jnp.concatenate([sct_ref[rg, chunk(c), :] for rg in range(tqs // LANES)], axis=1)

    lax.fori_loop(0, nch, score_chunk, 0)

    def fold8(x, op):
        pair = {jnp.sum: jnp.add, jnp.min: jnp.minimum, jnp.max: jnp.maximum}[op]
        parts = [x[8 * r:8 * (r + 1)] for r in range(tkc // 8)]
        while len(parts) > 1:
            odd = parts[len(parts) - 1:] if len(parts) % 2 else []
            parts = [pair(parts[r], parts[r + 1]) for r in range(0, len(parts) - 1, 2)] + odd
        return parts[0]

    def count_where(pred):
        def body(c, acc):
            m = jnp.where(pred(keys_t(c), c), 1.0, 0.0)
            return acc + fold8(m, jnp.sum)
        acc = lax.fori_loop(0, nch, body, jnp.zeros((8, tqs), F32))
        return jnp.sum(acc, axis=0, keepdims=True)

    def count_gt(t):
        return count_where(lambda x, c: x > t)

    def search():
        lane_q = lax.broadcasted_iota(jnp.int32, (1, tqs), 1)
        nvis = jnp.where(lane_q < tq, _visible_limit(q0 + lane_q, s_valid), 0).astype(F32)

        def mm_body(c, carry):
            mn, mx = carry
            x = keys_t(c)
            xm = jnp.where(x == KEY_NONE, KEY_MAX, x)
            return (jnp.minimum(mn, fold8(xm, jnp.min)), jnp.maximum(mx, fold8(x, jnp.max)))
        mn, mx = lax.fori_loop(0, nch, mm_body, (jnp.full((8, tqs), KEY_MAX, jnp.int32),
                                                 jnp.full((8, tqs), KEY_NONE, jnp.int32)))
        mn = jnp.min(mn, axis=0, keepdims=True)
        mx = jnp.max(mx, axis=0, keepdims=True)
        done0 = jnp.where(nvis <= kf, 1.0, 0.0)
        mn = jnp.where(nvis > 0.5, mn, 0)
        mx = jnp.where(nvis > 0.5, mx, 0)
        lo0 = mn - 1
        zeros = jnp.zeros((1, tqs), F32)
        izeros = jnp.zeros((1, tqs), jnp.int32)

        def snap(lo, hi, clo, chi, t, done, tie, vt):
            def body(c, carry):
                a, b = carry
                x = keys_t(c)
                xa = jnp.where(x > lo, x, KEY_MAX)
                xb = jnp.where(x <= hi, x, KEY_NONE)
                return (jnp.minimum(a, fold8(xa, jnp.min)), jnp.maximum(b, fold8(xb, jnp.max)))
            a, b = lax.fori_loop(0, nch, body, (jnp.full((8, tqs), KEY_MAX, jnp.int32),
                                                jnp.full((8, tqs), KEY_NONE, jnp.int32)))
            a = jnp.min(a, axis=0, keepdims=True)
            b = jnp.max(b, axis=0, keepdims=True)
            nd = done < 0.5
            open_ = nd & (a < b)
            b1 = jnp.where(open_, b - 1, hi)
            a0 = jnp.where(open_, a, lo)
            cb = count_gt(b1)
            ca = count_gt(a0)
            b_hit = open_ & (cb == kf)
            b_tie = open_ & (cb > kf)
            rest = open_ & (cb < kf)
            a_hit = rest & (ca == kf)
            a_tie = rest & (ca < kf)
            rest = rest & (ca > kf)
            new_tie = (nd & (a >= b)) | b_tie | a_tie
            t = jnp.where(b_hit, b1, jnp.where(a_hit, a0, t))
            vt = jnp.where(a_tie, a, jnp.where(new_tie, b, vt))
            tie = jnp.where(new_tie, 1.0, tie)
            done = jnp.where(new_tie | b_hit | a_hit, 1.0, done)
            lo = jnp.where(rest, a0, lo)
            clo = jnp.where(rest, ca, clo)
            hi = jnp.where(rest, b1, hi)
            chi = jnp.where(rest, cb, chi)
            return lo, hi, clo, chi, t, done, tie, vt

        def cond(s):
            return (s[1] > 0) & (s[0] < SEARCH_MAX_IT)

        def one_pass(it, s):
            lo, hi, clo, chi, t, done, tie, vt = s
            stuck = hi <= lo + 1
            lo_f, hi_f = _order_key_inv(lo), _order_key_inv(hi)
            itp = lo_f + (hi_f - lo_f) * ((clo - kf) / (clo - chi))
            cand = _order_key(jnp.where(it % 2 == 1, itp, 0.5 * lo_f + 0.5 * hi_f))
            imid = (lo & hi) + jnp.right_shift(lo ^ hi, 1)
            mid = jnp.where((cand > lo) & (cand < hi), cand, imid)
            c = count_gt(mid)
            nd = done < 0.5
            live = nd & jnp.logical_not(stuck)
            hit = live & (c == kf)
            stk = nd & stuck
            up = live & (c > kf)
            dn = live & (c < kf)
            t = jnp.where(hit, mid, t)
            vt = jnp.where(stk, hi, vt)
            tie = jnp.where(stk, 1.0, tie)
            lo = jnp.where(up, mid, lo)
            clo = jnp.where(up, c, clo)
            hi = jnp.where(dn, mid, hi)
            chi = jnp.where(dn, c, chi)
            done = jnp.where(hit | stk, 1.0, done)
            return lo, hi, clo, chi, t, done, tie, vt

        def body(s):
            it = s[0] + 2
            st = one_pass(it - 1, one_pass(it - 2, s[2:]))
            do_snap = (it >= SNAP_START) & ((it - SNAP_START) % SNAP_EVERY == 0)
            lo, hi, clo, chi, t, done, tie, vt = lax.cond(do_snap, snap, lambda *a: a, *st)
            active = (jnp.max(1.0 - done) > 0.5).astype(jnp.int32)
            return it, active, lo, hi, clo, chi, t, done, tie, vt

        active0 = (jnp.max(1.0 - done0) > 0.5).astype(jnp.int32)
        s = lax.while_loop(cond, body, (jnp.int32(0), active0, lo0, mx, nvis, zeros, lo0, done0,
                                        zeros, izeros))
        t, tie, vt = s[6], s[8], s[9]

        def tie_cut():
            t2 = jnp.where(tie > 0.5, vt, t)
            need = kf - count_gt(t2)

            def bs_body(_, carry):
                lo_c, hi_c = carry
                mid_c = jnp.floor((lo_c + hi_c) * 0.5)
                cnt = count_where(lambda x, c: (x == vt) & (key_pos_col(c) < mid_c))
                ok = cnt >= need
                return jnp.where(ok, lo_c, mid_c), jnp.where(ok, mid_c, hi_c)

            nbits = int(s_valid).bit_length()
            _, cut = lax.fori_loop(0, nbits, bs_body,
                                   (zeros, jnp.full((1, tqs), float(1 << nbits), F32)))
            return t2, jnp.where(tie > 0.5, cut, 0.0)

        any_tie = jnp.max(tie) > 0.5
        t, cut = lax.cond(any_tie, tie_cut, lambda: (t, zeros))
        return t, tie, vt, cut

    def no_search():
        z = jnp.zeros((1, tqs), F32)
        return (jnp.full((1, tqs), KEY_NONE, jnp.int32), z, jnp.zeros((1, tqs), jnp.int32), z)

    t, tie, vt, cut = lax.cond(lim_last > topk, search, no_search)

    def to_col(v):
        groups = [jnp.broadcast_to(v[:, rg * LANES:(rg + 1) * LANES], (LANES, LANES)).T
                  for rg in range(tqs // LANES)]
        full = groups[0] if len(groups) == 1 else jnp.concatenate(groups, axis=0)
        return full[:tq]

    t, vt, cut = to_col(t), to_col(vt), to_col(cut)
    tied = to_col(tie) > 0.5
    lane_f = lax.broadcasted_iota(jnp.int32, (1, LANES), 1).astype(F32)

    def to_mask(c, carry):
        for g in range(ngrp):
            cols = pl.ds(pl.multiple_of(c * tkc + g * LANES, LANES), LANES)
            x = sc_ref[:, cols]
            kpos = (c * tkc + g * LANES).astype(F32) + lane_f
            sel = (x > t) | (tied & (x == vt) & (kpos < cut))
            msk_ref[:, cols] = jnp.where(sel, 0.0, NEG)
        return carry

    lax.fori_loop(0, nch, to_mask, 0)

    bq = (bq_ref[...] * (B_SCALE * LOG2E)).astype(BF16)
    qbs = [bq[:, h * B_DIM:(h + 1) * B_DIM] for h in range(B_HEADS)]
    d0 = q0 // LANES

    def att_step(c, carries, nw):
        ks = pl.ds(pl.multiple_of(c * tkc, tkc), nw * tkc)
        msk = msk_ref[:, ks]
        rows = min(tq, LANES)
        out = []
        for h in range(B_HEADS):
            s = lax.dot_general(qbs[h], bk_ref[h, ks, :], _NT, preferred_element_type=F32)
            bias = jnp.concatenate(
                [jnp.concatenate([bias_ref[jnp.clip(d0 + rg - (c * ngrp + g), 0, 2), h, :rows, :]
                                  for g in range(nw * ngrp)], axis=1)
                 for rg in range(-(-tq // LANES))], axis=0)
            out.append(_online(carries[h], s + (msk + bias), bv_ref[h, ks, :]))
        return tuple(out)

    init = tuple(_softmax_init(tq, B_DIM) for _ in range(B_HEADS))
    nquad = nch // 4
    npair = (nch - 4 * nquad) // 2
    carries = lax.fori_loop(0, nquad, lambda j, cs: att_step(4 * j, cs, 4), init)
    carries = lax.fori_loop(0, npair, lambda j, cs: att_step(4 * nquad + 2 * j, cs, 2), carries)
    carries = lax.fori_loop(4 * nquad + 2 * npair, nch, lambda c, cs: att_step(c, cs, 1), carries)
    for h in range(B_HEADS):
        _, l, acc = carries[h]
        o_ref[:, h * B_DIM:(h + 1) * B_DIM] = acc / l


def _dsa(proj3, bk, bv, ki, bias, tq, tkc, past, s_valid, topk):
    b, t, _ = proj3.shape
    s_pad = bk.shape[2]
    tqs = -(-tq // LANES) * LANES
    assert past % LANES == 0 and (tq % LANES == 0 or t == tq) and s_pad % tkc == 0
    kern = functools.partial(_dsa_kernel, tq=tq, tqs=tqs, tkc=tkc, past=past, s_valid=s_valid,
                             topk=topk)
    return pl.pallas_call(
        kern,
        out_shape=jax.ShapeDtypeStruct((b, t, 256), F32),
        grid=(b, t // tq),
        in_specs=[pl.BlockSpec((None, tq, 256), lambda bi, i: (bi, i, CB_BQ)),
                  pl.BlockSpec((None, tq, 256), lambda bi, i: (bi, i, CB_IDX)),
                  pl.BlockSpec((None, B_HEADS, s_pad, B_DIM), lambda bi, i: (bi, 0, 0, 0)),
                  pl.BlockSpec((None, B_HEADS, s_pad, B_DIM), lambda bi, i: (bi, 0, 0, 0)),
                  pl.BlockSpec((None, s_pad, LANES), lambda bi, i: (bi, 0, 0)),
                  pl.BlockSpec((3, B_HEADS, LANES, LANES), lambda bi, i: (0, 0, 0, 0))],
        out_specs=pl.BlockSpec((None, tq, 256), lambda bi, i: (bi, i, 0)),
        scratch_shapes=[pltpu.VMEM((tq, s_pad), jnp.int32), pltpu.VMEM((tqs // LANES, s_pad, LANES), jnp.int32),
                        pltpu.VMEM((tq, s_pad), F32)],
        compiler_params=_cparams(("parallel", "arbitrary")),
        name="dsa_attn",
    )(proj3, proj3, bk, bv, ki, bias)


def _pool_kernel(u_ref, prev_ref, w_ref, scale_ref, o_ref, st_ref, ext_ref, *, tr, past):
    i = pl.program_id(1)

    @pl.when(i == 0)
    def _():
        ext_ref[1:16, :] = prev_ref[...]

    u = u_ref[...]
    ext_ref[16:16 + tr, :] = u

    def tap(k, lo):
        return ext_ref[16 - k:16 - k + tr, lo:lo + LANES]

    pos1 = past + i * tr + 1 + lax.broadcasted_iota(jnp.int32, (tr, 1), 0)
    cnt = [jnp.minimum(w, pos1).astype(F32) for w in POOL_WINDOWS]
    lane = lax.broadcasted_iota(jnp.int32, (1, LANES), 1)
    first = lane < POOL_GROUP

    s2 = tap(0, 0) + tap(1, 0)
    s4 = s2 + tap(2, 0) + tap(3, 0)
    s8 = tap(0, LANES)
    for k in range(1, 8):
        s8 = s8 + tap(k, LANES)
    s16 = s8
    for k in range(8, 16):
        s16 = s16 + tap(k, LANES)
    m_lo = jnp.where(first, s2 / cnt[0], s4 / cnt[1])
    m_hi = jnp.where(first, s8 / cnt[2], s16 / cnt[3])
    pooled = jnp.concatenate([m_lo, m_hi], axis=1) - u
    mixed = jnp.dot(pooled.astype(BF16), w_ref[...], preferred_element_type=F32)
    o_ref[...] = mixed * scale_ref[...]

    new_state = ext_ref[1 + tr:16 + tr, :]
    st_ref[...] = new_state
    ext_ref[1:16, :] = new_state


def _pool(proj3, prev, w_bd, scale, tr, past):
    b, t, _ = proj3.shape
    kern = functools.partial(_pool_kernel, tr=tr, past=past)
    return pl.pallas_call(
        kern,
        out_shape=(jax.ShapeDtypeStruct((b, t, POOL_WIDTH), F32),
                   jax.ShapeDtypeStruct((b, POOL_STATE, POOL_WIDTH), F32)),
        grid=(b, t // tr),
        in_specs=[pl.BlockSpec((None, tr, 256), lambda bi, i: (bi, i, CB_POOL)),
                  pl.BlockSpec((None, POOL_STATE, POOL_WIDTH), lambda bi, i: (bi, 0, 0)),
                  pl.BlockSpec((256, 256), lambda bi, i: (0, 0)),
                  pl.BlockSpec((1, 256), lambda bi, i: (0, 0))],
        out_specs=(pl.BlockSpec((None, tr, 256), lambda bi, i: (bi, i, 0)),
                   pl.BlockSpec((None, POOL_STATE, POOL_WIDTH), lambda bi, i: (bi, 0, 0))),
        scratch_shapes=[pltpu.VMEM((tr + 16, POOL_WIDTH), F32)],
        compiler_params=_cparams(("parallel", "arbitrary")),
        name="pool_mix",
    )(proj3, prev, w_bd, scale)


def _conv_kernel(u_ref, prev_ref, w_ref, b_ref, g_ref, beta_ref, o_ref, st_ref, ext_ref, sh_ref,
                 *, tr, sub):
    i = pl.program_id(1)
    npre = CONV_K - 1

    @pl.when(i == 0)
    def _():
        ext_ref[2:2 + npre, :] = prev_ref[...]

    u = u_ref[...]
    ext_ref[32:32 + tr, :] = u[:, :CONV_WIDTH] * _sigmoid(u[:, CONV_WIDTH:])

    for r in range(8):
        n = tr + 8 * ((CONV_K - 1 - r) // 8)
        sh_ref[r, 0:n, :] = ext_ref[2 + r:2 + r + n, :]

    for r0 in range(0, tr, sub):
        acc = None
        for k in range(CONV_K):
            a, r = divmod(k, 8)
            term = sh_ref[r, r0 + 8 * a:r0 + 8 * a + sub, :] * w_ref[k:k + 1, :]
            acc = term if acc is None else acc + term
        y = _ln_rows(acc + b_ref[...], g_ref[...], beta_ref[...])
        o_ref[r0:r0 + sub, :] = y * _sigmoid(y)

    new_state = ext_ref[2 + tr:2 + tr + npre, :]
    st_ref[...] = new_state
    ext_ref[2:2 + npre, :] = new_state


def _conv(proj3, prev, w, bias, g, beta, tr, sub):
    b, t, _ = proj3.shape
    npre = CONV_K - 1
    kern = functools.partial(_conv_kernel, tr=tr, sub=sub)
    vec = pl.BlockSpec((1, CONV_WIDTH), lambda bi, i: (0, 0))
    return pl.pallas_call(
        kern,
        out_shape=(jax.ShapeDtypeStruct((b, t, CONV_WIDTH), F32),
                   jax.ShapeDtypeStruct((b, npre, CONV_WIDTH), F32)),
        grid=(b, t // tr),
        in_specs=[pl.BlockSpec((None, tr, 512), lambda bi, i: (bi, i, CB_CONV)),
                  pl.BlockSpec((None, npre, CONV_WIDTH), lambda bi, i: (bi, 0, 0)),
                  pl.BlockSpec((CONV_K, CONV_WIDTH), lambda bi, i: (0, 0)),
                  vec, vec, vec],
        out_specs=(pl.BlockSpec((None, tr, CONV_WIDTH), lambda bi, i: (bi, i, 0)),
                   pl.BlockSpec((None, npre, CONV_WIDTH), lambda bi, i: (bi, 0, 0))),
        scratch_shapes=[pltpu.VMEM((tr + 32, CONV_WIDTH), F32),
                        pltpu.VMEM((8, tr + 24, CONV_WIDTH), F32)],
        compiler_params=_cparams(("parallel", "arbitrary")),
        name="conv_module",
    )(proj3, prev, w, bias, g, beta)


def _merge_kernel(a_ref, b_ref, c_ref, d_ref, x_ref, wg_ref, wb_ref, wo_ref, g_ref, beta_ref,
                  o_ref):
    x = x_ref[...]
    xb = x.astype(BF16)
    mixed = None
    for n, br_ref in enumerate((a_ref, b_ref, c_ref, d_ref)):
        gate = jnp.dot(xb, wg_ref[:, n * D_MODEL:(n + 1) * D_MODEL], preferred_element_type=F32)
        br = jnp.dot(br_ref[...].astype(BF16), wb_ref[n], preferred_element_type=F32)
        term = _sigmoid(gate) * br
        mixed = term if mixed is None else mixed + term
    y = jnp.dot(mixed.astype(BF16), wo_ref[...], preferred_element_type=F32)
    o_ref[...] = _ln_rows(ALPHA * x + y, g_ref[...], beta_ref[...])


def _merge(oa, ob, oc, od, x3, wg, wb, wo, g, beta, tm):
    b, t, _ = x3.shape
    br_spec = pl.BlockSpec((None, tm, 256), lambda bi, i: (bi, i, 0))
    const = lambda shape: pl.BlockSpec(shape, lambda bi, i: (0,) * len(shape),
                                       pipeline_mode=pl.Buffered(1))
    return pl.pallas_call(
        _merge_kernel,
        out_shape=jax.ShapeDtypeStruct((b, t, D_MODEL), F32),
        grid=(b, t // tm),
        in_specs=[br_spec, br_spec, br_spec, br_spec,
                  pl.BlockSpec((None, tm, D_MODEL), lambda bi, i: (bi, i, 0)),
                  const((D_MODEL, N_BRANCH * D_MODEL)), const((N_BRANCH, 256, D_MODEL)),
                  const((D_MODEL, D_MODEL)), const((1, D_MODEL)), const((1, D_MODEL))],
        out_specs=pl.BlockSpec((None, tm, D_MODEL), lambda bi, i: (bi, i, 0)),
        compiler_params=_cparams(("parallel", "parallel")),
        name="merge_ln1",
    )(oa, ob, oc, od, x3, wg, wb, wo, g, beta)


FF_TC = 256
FF_NC = D_FF // FF_TC


def _ffn_kernel(x_ref, st_ref, wup_ref, cw_ref, cb_ref, wd_ref, g_ref, beta_ref, o_ref, nst_ref,
                car_ref, ubuf_ref, h_ref, *, tm):
    @pl.when(pl.program_id(1) == 0)
    def _():
        car_ref[6:8, :] = st_ref[...]

    x = x_ref[...]
    xb = x.astype(BF16)

    def conv3(off, slot):
        cols = slice(off, off + FF_TC)
        u = jnp.dot(xb, wup_ref[:, cols], preferred_element_type=F32)
        ubuf_ref[slot, 6:8, :] = car_ref[6:8, cols]
        ubuf_ref[slot, 8:8 + tm, :] = u
        y = (cw_ref[0:1, cols] * ubuf_ref[slot, 6:6 + tm, :]
             + cw_ref[1:2, cols] * ubuf_ref[slot, 7:7 + tm, :]
             + cw_ref[2:3, cols] * u + cb_ref[:, cols])
        car_ref[6:8, cols] = ubuf_ref[slot, 6 + tm:8 + tm, :]
        return y

    for c in range(FF_NC):
        val = conv3(c * FF_TC, (2 * c) % 4)
        gate = conv3(D_FF + c * FF_TC, (2 * c + 1) % 4)
        h_ref[:, c * FF_TC:(c + 1) * FF_TC] = ((gate * _sigmoid(gate)) * val).astype(BF16)

    nst_ref[...] = car_ref[6:8, :]
    y = jnp.dot(h_ref[...], wd_ref[...], preferred_element_type=F32)
    o_ref[...] = _ln_rows(ALPHA * x + y, g_ref[...], beta_ref[...])


def _ffn(x3, st, w_up, cw, cb, w_dn, g, beta, tm):
    b, t, _ = x3.shape
    kern = functools.partial(_ffn_kernel, tm=tm)
    nk = FFN_K - 1
    const = lambda shape: pl.BlockSpec(shape, lambda bi, i: (0,) * len(shape),
                                       pipeline_mode=pl.Buffered(1))
    return pl.pallas_call(
        kern,
        out_shape=(jax.ShapeDtypeStruct((b, t, D_MODEL), F32),
                   jax.ShapeDtypeStruct((b, nk, 2 * D_FF), F32)),
        grid=(b, t // tm),
        in_specs=[pl.BlockSpec((None, tm, D_MODEL), lambda bi, i: (bi, i, 0)),
                  pl.BlockSpec((None, nk, 2 * D_FF), lambda bi, i: (bi, 0, 0)),
                  const((D_MODEL, 2 * D_FF)), const((FFN_K, 2 * D_FF)), const((1, 2 * D_FF)),
                  const((D_FF, D_MODEL)), const((1, D_MODEL)), const((1, D_MODEL))],
        out_specs=(pl.BlockSpec((None, tm, D_MODEL), lambda bi, i: (bi, i, 0)),
                   pl.BlockSpec((None, nk, 2 * D_FF), lambda bi, i: (bi, 0, 0))),
        scratch_shapes=[pltpu.VMEM((8, 2 * D_FF), F32),
                        pltpu.VMEM((4, tm + 8, FF_TC), F32),
                        pltpu.VMEM((tm, D_FF), BF16)],
        compiler_params=_cparams(("parallel", "arbitrary")),
        name="conv_ffn_ln2",
    )(x3, st, w_up, cw, cb, w_dn, g, beta)


def _rope_tables(past, t):
    half = A_ROPE // 2
    freqs = ROPE_BASE ** (-jnp.arange(half, dtype=F32) / half)
    pos = past + jnp.arange(t, dtype=jnp.int32)
    ang = pos.astype(F32)[:, None] * freqs[None, :]
    cos, sin = jnp.cos(ang), jnp.sin(ang)
    c2 = jnp.concatenate([cos, cos], axis=1)
    s2 = jnp.concatenate([sin, sin], axis=1)
    one = jnp.ones((t, A_NOPE), F32)
    z = lambda n: jnp.zeros((t, n), F32)
    cosq = jnp.concatenate([one, c2, z(LANES - A_NOPE - A_ROPE)], axis=1)
    sinq = jnp.concatenate([z(A_NOPE), s2, z(LANES - A_NOPE - A_ROPE)], axis=1)
    cosk = jnp.concatenate([c2, z(LANES - A_ROPE)], axis=1)
    sink = jnp.concatenate([s2, z(LANES - A_ROPE)], axis=1)
    return cosq, sinq, cosk, sink


_BUCKET_EDGES = (12, 16, 23, 32, 46, 64, 91)


def _rel_bucket(rel):
    nb = REL_BUCKETS // 2
    max_exact = nb // 2
    n = np.abs(rel)
    large = max_exact + sum((n >= e).astype(np.int64) for e in _BUCKET_EDGES)
    return np.where(rel > 0, nb, 0) + np.where(n < max_exact, n, large)


def _bias_tiles(rel_bias):
    r = np.arange(LANES)[:, None]
    c = np.arange(LANES)[None, :]
    buckets = np.stack([_rel_bucket(c - r - LANES * d) for d in range(3)])
    tiles = jnp.zeros((B_HEADS,) + buckets.shape, F32)
    for bkt in np.unique(buckets):
        sel = jnp.asarray(buckets == bkt)
        tiles = jnp.where(sel[None], rel_bias[int(bkt)][:, None, None, None], tiles)
    return jnp.transpose(tiles, (1, 0, 2, 3)) * LOG2E


def _pack_w_in(w):
    z = lambda n: jnp.zeros((D_MODEL, n), F32)
    kr = w[:, O_KROPE:O_KROPE + A_ROPE]
    half = A_ROPE // 2
    kr_rot = jnp.concatenate([-kr[:, half:], kr[:, :half]], axis=1)
    parts = [
        w[:, O_CONV:O_CONV + 2 * CONV_WIDTH],
        w[:, O_CQ:O_CQ + A_Q_LORA], z(256 - A_Q_LORA),
        w[:, O_CKV:O_CKV + A_KV_LORA], kr, kr_rot, z(256 - A_KV_LORA - 2 * A_ROPE),
        w[:, O_BQ:O_BQ + 256], w[:, O_BK:O_BK + 256], w[:, O_BV:O_BV + 256],
        w[:, O_QIDX:O_QIDX + 128], w[:, O_KIDX:O_KIDX + IDX_DIM], w[:, O_WIDX:O_WIDX + IDX_HEADS],
        z(256 - 128 - IDX_DIM - IDX_HEADS),
        w[:, O_POOL:O_POOL + POOL_WIDTH],
    ]
    return jnp.concatenate(parts, axis=1).astype(BF16)


def _pack_w_qup(w):
    zrow = lambda m: jnp.zeros((256 - A_Q_LORA, m.shape[1]), F32)
    half = A_ROPE // 2
    dq = A_NOPE + A_ROPE
    cols, rcols = [], []
    for h in range(A_HEADS):
        nope = w[:, h * dq:h * dq + A_NOPE]
        rp = w[:, h * dq + A_NOPE:(h + 1) * dq]
        pad = jnp.zeros((A_Q_LORA, LANES - dq), F32)
        cols += [nope, rp, pad]
        rcols += [jnp.zeros_like(nope), -rp[:, half:], rp[:, :half], pad]
    wq = jnp.concatenate(cols, axis=1)
    wqr = jnp.concatenate(rcols, axis=1)
    wq = jnp.concatenate([wq, zrow(wq)], axis=0).astype(BF16)
    wqr = jnp.concatenate([wqr, zrow(wqr)], axis=0).astype(BF16)
    return wq, wqr


def _pack_w_kvup(w):
    dk = A_NOPE + A_V
    kc, vc = [], []
    for h in range(A_HEADS):
        kc += [w[:, h * dk:h * dk + A_NOPE], jnp.zeros((A_KV_LORA, LANES - A_NOPE), F32)]
        vc += [w[:, h * dk + A_NOPE:(h + 1) * dk]]
    return jnp.concatenate(kc, axis=1).astype(BF16), jnp.concatenate(vc, axis=1).astype(BF16)


def _rope_placement():
    e = np.zeros((LANES, A_HEADS * LANES), np.float32)
    for h in range(A_HEADS):
        for r in range(A_ROPE):
            e[r, h * LANES + A_NOPE + r] = 1.0
    return jnp.asarray(e, dtype=BF16)


def _pool_blockdiag(pw):
    out = jnp.zeros((POOL_WIDTH, POOL_WIDTH), F32)
    for g in range(len(POOL_WINDOWS)):
        out = out.at[g * POOL_GROUP:(g + 1) * POOL_GROUP, g * POOL_GROUP:(g + 1) * POOL_GROUP].set(pw[g])
    return out.astype(BF16)


class _Tiles:
    def __init__(self, t):
        prompt = t > 128
        self.rows = 1024 if prompt else 128
        self.in_tn = 1152
        self.prep = 512 if prompt else t
        self.kv = 512
        self.mla_q = 1024 if prompt else t
        self.mla_k = 1024 if prompt else 17 * LANES
        self.dsa_q = 256 if prompt else t
        self.dsa_kc = 512 if prompt else 17 * LANES
        self.pool = 512 if prompt else t
        self.conv = 256 if prompt else t
        self.conv_sub = 64 if prompt else t
        self.merge = 512 if prompt else t
        self.ffn = 512 if prompt else t
        self.s_align = 512 if prompt else 17 * LANES


def _trunk_layer(x3, caches, lw, consts, past, stacks, layer):
    (w_in, w_gates, gq, gkv, wq, wqr, wkn, wv, pool_bd, pool_scale, conv_w, conv_b, conv_g, conv_beta,
     w_branch, w_out, ln1_g, ln1_b, w_up, ffn_cw, ffn_cb, w_down, ln2_g, ln2_b) = lw
    tabs, bias, e_place = consts
    b, t, _ = x3.shape
    tl = _Tiles(t)
    s_valid = past + t
    s_pad = -(-s_valid // tl.s_align) * tl.s_align
    topk = min(TOPK_MAX, s_valid // 4)

    proj = _inproj(x3.reshape(b * t, D_MODEL), w_in, tl.rows, tl.in_tn)
    proj3 = proj.reshape(b, t, PROJ_W)
    (qa, kr128, bk_hm, bv_hm, ki_rep), stacks = _aprep(
        proj3, tabs, gq, gkv, wq, wqr, stacks, layer, tl.prep)
    ckv_n = stacks[0][layer]

    def keys(cache, new, width):
        parts = [new] if cache is None else [cache.reshape(b, past, -1), new]
        if s_pad > s_valid:
            parts.append(jnp.zeros((b, s_pad - s_valid, width), new.dtype))
        return parts[0] if len(parts) == 1 else jnp.concatenate(parts, axis=1)

    c_ckv, c_krope, c_bk, c_bv, c_kidx, st_pool, st_conv, st_ffn = caches
    ckv_all = keys(c_ckv, ckv_n, A_KV_LORA)
    if c_krope is None:
        kr_all = keys(None, kr128, LANES)
    else:
        c_kr128 = jnp.pad(c_krope, ((0, 0), (0, 0), (0, LANES - A_ROPE)))
        kr_all = keys(c_kr128, kr128, LANES)
    if c_bk is None and s_pad == s_valid:
        bk_all, bv_all, ki_all = bk_hm, bv_hm, ki_rep
    else:
        head_major = lambda a: jnp.transpose(a.astype(BF16).reshape(b, s_pad, B_HEADS, B_DIM),
                                             (0, 2, 1, 3))
        bk_all = head_major(keys(c_bk, stacks[2][layer], 256))
        bv_all = head_major(keys(c_bv, stacks[3][layer], 256))
        ki_all = jnp.tile(keys(c_kidx, stacks[4][layer], IDX_DIM).astype(BF16), (1, 1, IDX_HEADS))

    kx, vx = _kvup(ckv_all.reshape(b * s_pad, A_KV_LORA), kr_all.reshape(b * s_pad, LANES),
                   wkn, wv, e_place, tl.kv)
    out_a = _mla(qa, kx.reshape(b, s_pad, 512), vx.reshape(A_HEADS, b, s_pad, A_V),
                 tl.mla_q, tl.mla_k, past, s_valid)
    out_b = _dsa(proj3, bk_all, bv_all, ki_all, bias, tl.dsa_q, tl.dsa_kc, past, s_valid, topk)
    out_c, new_pool = _pool(proj3, st_pool, pool_bd, pool_scale, tl.pool, past)
    out_d, new_conv = _conv(proj3, st_conv, conv_w, conv_b, conv_g, conv_beta, tl.conv, tl.conv_sub)
    x1 = _merge(out_a, out_b, out_c, out_d, x3, w_gates, w_branch, w_out, ln1_g, ln1_b, tl.merge)
    x2, new_ffn = _ffn(x1, st_ffn, w_up, ffn_cw, ffn_cb, w_down, ln2_g, ln2_b, tl.ffn)
    return x2, stacks, (new_pool, new_conv, new_ffn)


def kernel(x_prompt, x_sample, cache_a_ckv, cache_a_krope, cache_b_k, cache_b_v, cache_b_kidx,
           state_pool, state_conv, state_ffn, rel_bias, ln_in_g, ln_in_b, w_in, a_q_norm, a_kv_norm,
           a_w_qup, a_w_kvup, pool_w, pool_scale, conv_w, conv_b, conv_ln_g, conv_ln_b, w_branch,
           w_out, ln1_g, ln1_b, w_up, ffn_conv_w, ffn_conv_b, w_down, ln2_g, ln2_b):
    bp, tp, _ = x_prompt.shape
    bs, ts, _ = x_sample.shape
    past = cache_a_ckv.shape[2]
    depth = w_in.shape[0]

    xp = _layer_norm(x_prompt.reshape(bp * tp, D_MODEL), ln_in_g, ln_in_b,
                     _Tiles(tp).rows).reshape(bp, tp, D_MODEL)
    xs = _layer_norm(x_sample.reshape(bs * ts, D_MODEL), ln_in_g, ln_in_b,
                     _Tiles(ts).rows).reshape(bs, ts, D_MODEL)

    e_place = _rope_placement()
    bias = _bias_tiles(rel_bias)
    consts_p = (_rope_tables(0, tp), bias, e_place)
    consts_s = (_rope_tables(past, ts), bias, e_place)
    zeros_p = (None, None, None, None, None,
               jnp.zeros((bp, POOL_STATE, POOL_WIDTH), F32),
               jnp.zeros((bp, CONV_K - 1, CONV_WIDTH), F32),
               jnp.zeros((bp, FFN_K - 1, 2 * D_FF), F32))

    def state_stacks(b, t):
        return tuple(jnp.zeros((depth, b, t, w), F32)
                     for w in (A_KV_LORA, A_ROPE, 256, 256, IDX_DIM))

    stacks_p, stacks_s = state_stacks(bp, tp), state_stacks(bs, ts)
    row = lambda v: v.reshape(1, -1)
    p_states, s_states = [], []
    for l in range(depth):
        wq, wqr = _pack_w_qup(a_w_qup[l])
        wkn, wv = _pack_w_kvup(a_w_kvup[l])
        gq = jnp.concatenate([a_q_norm[l], jnp.zeros((256 - A_Q_LORA,), F32)]).reshape(1, 256)
        w_gates = w_in[l][:, O_GATES:O_GATES + N_BRANCH * D_MODEL].astype(BF16)
        lw = (_pack_w_in(w_in[l]), w_gates, gq, row(a_kv_norm[l]), wq, wqr, wkn, wv,
              _pool_blockdiag(pool_w[l]), row(pool_scale[l]), conv_w[l], row(conv_b[l]),
              row(conv_ln_g[l]), row(conv_ln_b[l]), w_branch[l].astype(BF16), w_out[l].astype(BF16),
              row(ln1_g[l]), row(ln1_b[l]), w_up[l].astype(BF16), ffn_conv_w[l], row(ffn_conv_b[l]),
              w_down[l].astype(BF16), row(ln2_g[l]), row(ln2_b[l]))
        xp, stacks_p, st_p = _trunk_layer(xp, zeros_p, lw, consts_p, 0, stacks_p, l)
        caches_s = (cache_a_ckv[l], cache_a_krope[l], cache_b_k[l], cache_b_v[l], cache_b_kidx[l],
                    state_pool[l], state_conv[l], state_ffn[l])
        xs, stacks_s, st_s = _trunk_layer(xs, caches_s, lw, consts_s, past, stacks_s, l)
        p_states.append(st_p)
        s_states.append(st_s)

    def outputs(stacks, small, b, t):
        ckv, krope, bk, bv, kidx = stacks
        heads = lambda a: a.reshape(depth, b, t, B_HEADS, B_DIM)
        return [ckv, krope, heads(bk), heads(bv), kidx] + [jnp.stack(a) for a in zip(*small)]

    return (xp, xs, *outputs(stacks_p, p_states, bp, tp), *outputs(stacks_s, s_states, bs, ts))
```

```python
import functools

import numpy as np
import jax
import jax.numpy as jnp
from jax import lax
from jax.experimental import pallas as pl
from jax.experimental.pallas import tpu as pltpu

F32 = jnp.float32
BF16 = jnp.bfloat16

D_MODEL = 1024
DEPTH = 4
CHUNK = 64
CHUNK_SHIFT = 6
N_BRANCH = 4
A_HEADS = 4
A_NOPE = 64
A_ROPE = 32
A_V = 64
A_Q_LORA = 192
A_KV_LORA = 128
A_SCALE = (A_NOPE + A_ROPE) ** -0.5
ROPE_BASE = 10000.0
B_HEADS = 4
B_DIM = 64
B_SCALE = B_DIM ** -0.5
IDX_HEADS = 4
IDX_DIM = 32
IDX_SCALE = (IDX_HEADS ** -0.5) * (IDX_DIM ** -0.5)
TOPK_MAX = 256
REL_BUCKETS = 32
POOL_WINDOWS = (2, 4, 8, 16)
POOL_WIDTH = 256
POOL_GROUP = 64
POOL_STATE = 15
CONV_WIDTH = 256
CONV_K = 31
D_FF = 2816
FFN_K = 3
ALPHA = (2 * DEPTH) ** 0.25
LN_EPS = 1e-5

IN_SPLITS = (A_Q_LORA, A_KV_LORA, A_ROPE, 256, 256, 256, 128, IDX_DIM, IDX_HEADS,
             POOL_WIDTH, 2 * CONV_WIDTH, N_BRANCH * D_MODEL)
_OFF = [0] + [int(v) for v in np.cumsum(IN_SPLITS)]
(O_CQ, O_CKV, O_KROPE, O_BQ, O_BK, O_BV, O_QIDX, O_KIDX, O_WIDX, O_POOL, O_CONV, O_GATES,
 D_IN) = _OFF

PROJ_W = 2304
CB_CONV = 0
CB_CQ = 2
CB_CK = 3
CB_BQ = 4
CB_BK = 5
CB_BV = 6
CB_IDX = 7
CB_POOL = 8

VMEM_LIMIT_BYTES = 48 * 1024 * 1024
LANES = 128

LOG2E = 1.4426950408889634
NEG = -1e30
M_INIT = -5e29
SEARCH_MAX_IT = 320
SNAP_START = 18
SNAP_EVERY = 2

_NT = (((1,), (1,)), ((), ()))


def _cparams(sem):
    return pltpu.CompilerParams(dimension_semantics=sem, vmem_limit_bytes=VMEM_LIMIT_BYTES)


def _ln_rows(x, g, b):
    mu = jnp.mean(x, axis=-1, keepdims=True)
    xc = x - mu
    var = jnp.mean(xc * xc, axis=-1, keepdims=True)
    return xc * lax.rsqrt(var + LN_EPS) * g + b


def _sigmoid(x):
    return 1.0 / (1.0 + jnp.exp(-x))


def _ln_kernel(x_ref, g_ref, b_ref, o_ref):
    o_ref[...] = _ln_rows(x_ref[...], g_ref[...], b_ref[...])


def _layer_norm(x2d, g, b, tm):
    n, d = x2d.shape
    return pl.pallas_call(
        _ln_kernel,
        out_shape=jax.ShapeDtypeStruct((n, d), F32),
        grid=(n // tm,),
        in_specs=[pl.BlockSpec((tm, d), lambda i: (i, 0)),
                  pl.BlockSpec((1, d), lambda i: (0, 0)),
                  pl.BlockSpec((1, d), lambda i: (0, 0))],
        out_specs=pl.BlockSpec((tm, d), lambda i: (i, 0)),
        compiler_params=_cparams(("parallel",)),
        name="input_ln",
    )(x2d, g.reshape(1, d), b.reshape(1, d))


def _inproj_kernel(x_ref, w_ref, o_ref, xb_ref):
    @pl.when(pl.program_id(1) == 0)
    def _():
        xb_ref[...] = x_ref[...].astype(BF16)

    o_ref[...] = jnp.dot(xb_ref[...], w_ref[...], preferred_element_type=F32)


def _inproj(x2d, w, tm, tn):
    n, k = x2d.shape
    nw = w.shape[1]
    return pl.pallas_call(
        _inproj_kernel,
        out_shape=jax.ShapeDtypeStruct((n, nw), F32),
        grid=(n // tm, nw // tn),
        in_specs=[pl.BlockSpec((tm, k), lambda i, j: (i, 0)),
                  pl.BlockSpec((k, tn), lambda i, j: (0, j))],
        out_specs=pl.BlockSpec((tm, tn), lambda i, j: (i, j)),
        scratch_shapes=[pltpu.VMEM((tm, k), BF16)],
        compiler_params=_cparams(("parallel", "arbitrary")),
        name="in_proj",
    )(x2d, w)


def _aprep_kernel(cq_ref, ck_ref, bk_ref, bv_ref, idx_ref, cosq_ref, sinq_ref, cosk_ref, sink_ref,
                  gq_ref, gkv_ref, wq_ref, wqr_ref, _ckv_in, _kro_in, _sbk_in, _sbv_in, _ski_in,
                  qa_ref, kr_ref, hbk_ref, hbv_ref, rki_ref,
                  ckv_ref, kro_ref, sbk_ref, sbv_ref, ski_ref):
    bk = bk_ref[...]
    bv = bv_ref[...]
    sbk_ref[...] = bk
    sbv_ref[...] = bv
    bkb = bk.astype(BF16)
    bvb = bv.astype(BF16)
    for h in range(B_HEADS):
        hbk_ref[h] = bkb[:, h * B_DIM:(h + 1) * B_DIM]
        hbv_ref[h] = bvb[:, h * B_DIM:(h + 1) * B_DIM]
    ki = idx_ref[:, LANES:]
    ski_ref[...] = ki[:, :IDX_DIM]
    lane = lax.broadcasted_iota(jnp.int32, (1, LANES), 1)
    k0 = jnp.where(lane < IDX_DIM, ki, 0.0)
    rep = k0
    for h in range(1, IDX_HEADS):
        rep = rep + pltpu.roll(k0, h * IDX_DIM, 1)
    rki_ref[...] = rep.astype(BF16)

    cq = cq_ref[...]
    ms = jnp.sum(cq * cq, axis=-1, keepdims=True) * (1.0 / A_Q_LORA)
    cqb = ((cq * lax.rsqrt(ms + LN_EPS)) * gq_ref[...]).astype(BF16)
    q = jnp.dot(cqb, wq_ref[...], preferred_element_type=F32)
    qr = jnp.dot(cqb, wqr_ref[...], preferred_element_type=F32)
    cos = jnp.concatenate([cosq_ref[...]] * A_HEADS, axis=1)
    sin = jnp.concatenate([sinq_ref[...]] * A_HEADS, axis=1)
    qa_ref[...] = ((q * cos + qr * sin) * (A_SCALE * LOG2E)).astype(BF16)

    ck = ck_ref[...]
    ckv = ck[:, :A_KV_LORA]
    ms2 = jnp.mean(ckv * ckv, axis=-1, keepdims=True)
    ckv_ref[...] = (ckv * lax.rsqrt(ms2 + LN_EPS)) * gkv_ref[...]
    kr = ck[:, A_KV_LORA:]
    krr = pltpu.roll(kr, LANES - A_ROPE, 1)
    kr_new = kr * cosk_ref[...] + krr * sink_ref[...]
    kr_ref[...] = kr_new
    kro_ref[...] = kr_new[:, :A_ROPE]


def _aprep(proj3, tabs, gq, gkv, wq, wqr, stacks, layer, tm):
    b, t, _ = proj3.shape
    cosq, sinq, cosk, sink = tabs
    tab_spec = pl.BlockSpec((tm, LANES), lambda bi, i: (i, 0))
    full = lambda shape: pl.BlockSpec(shape, lambda bi, i: (0,) * len(shape))
    blk = lambda cb: pl.BlockSpec((None, tm, 256), lambda bi, i: (bi, i, cb))
    rows = lambda w: pl.BlockSpec((None, tm, w), lambda bi, i: (bi, i, 0))
    heads = pl.BlockSpec((None, B_HEADS, tm, B_DIM), lambda bi, i: (bi, 0, i, 0))
    stacked = lambda a: pl.BlockSpec((None, None, tm, a.shape[-1]), lambda bi, i: (layer, bi, i, 0))
    untouched = pl.BlockSpec(memory_space=pl.ANY)
    n_in = 13
    out = pl.pallas_call(
        _aprep_kernel,
        out_shape=(jax.ShapeDtypeStruct((b, t, 512), BF16),
                   jax.ShapeDtypeStruct((b, t, LANES), F32),
                   jax.ShapeDtypeStruct((b, B_HEADS, t, B_DIM), BF16),
                   jax.ShapeDtypeStruct((b, B_HEADS, t, B_DIM), BF16),
                   jax.ShapeDtypeStruct((b, t, LANES), BF16))
        + tuple(jax.ShapeDtypeStruct(a.shape, a.dtype) for a in stacks),
        grid=(b, t // tm),
        in_specs=[blk(CB_CQ), blk(CB_CK), blk(CB_BK), blk(CB_BV), blk(CB_IDX),
                  tab_spec, tab_spec, tab_spec, tab_spec,
                  full((1, 256)), full((1, A_KV_LORA)), full((256, 512)), full((256, 512))]
        + [untouched] * len(stacks),
        out_specs=(rows(512), rows(LANES), heads, heads, rows(LANES))
        + tuple(stacked(a) for a in stacks),
        input_output_aliases={n_in + k: 5 + k for k in range(len(stacks))},
        compiler_params=_cparams(("parallel", "parallel")),
        name="mla_prep",
    )(proj3, proj3, proj3, proj3, proj3, cosq, sinq, cosk, sink, gq, gkv, wq, wqr, *stacks)
    return out[:5], tuple(out[5:])


def _kvup_kernel(ckv_ref, kr_ref, wkn_ref, wv_ref, e_ref, k_ref, v_ref):
    cb = ckv_ref[...].astype(BF16)
    k = jnp.dot(cb, wkn_ref[...], preferred_element_type=F32)
    k = k + jnp.dot(kr_ref[...].astype(BF16), e_ref[...], preferred_element_type=F32)
    k_ref[...] = k.astype(BF16)
    v = jnp.dot(cb, wv_ref[...], preferred_element_type=F32).astype(BF16)
    for h in range(A_HEADS):
        v_ref[h] = v[:, h * A_V:(h + 1) * A_V]


def _kvup(ckv2d, kr2d, wkn, wv, e, tm):
    m = ckv2d.shape[0]
    full = lambda shape: pl.BlockSpec(shape, lambda i: (0,) * len(shape))
    return pl.pallas_call(
        _kvup_kernel,
        out_shape=(jax.ShapeDtypeStruct((m, 512), BF16),
                   jax.ShapeDtypeStruct((A_HEADS, m, A_V), BF16)),
        grid=(m // tm,),
        in_specs=[pl.BlockSpec((tm, A_KV_LORA), lambda i: (i, 0)),
                  pl.BlockSpec((tm, LANES), lambda i: (i, 0)),
                  full((A_KV_LORA, 512)), full((A_KV_LORA, 256)), full((LANES, 512))],
        out_specs=(pl.BlockSpec((tm, 512), lambda i: (i, 0)),
                   pl.BlockSpec((A_HEADS, tm, A_V), lambda i: (0, i, 0))),
        compiler_params=_cparams(("parallel",)),
        name="kv_up",
    )(ckv2d, kr2d, wkn, wv, e)


def _online(carry, s, v):
    m, l, acc = carry
    m_new = jnp.maximum(m, jnp.max(s, axis=1, keepdims=True))
    alpha = jnp.exp2(m - m_new)
    p = jnp.exp2(s - m_new)
    l = alpha * l + jnp.sum(p, axis=1, keepdims=True)
    acc = alpha * acc + jnp.dot(p.astype(BF16), v, preferred_element_type=F32)
    return m_new, l, acc


def _softmax_init(tq, dv):
    return (jnp.full((tq, 1), M_INIT, F32), jnp.zeros((tq, 1), F32), jnp.zeros((tq, dv), F32))


def _visible_limit(qpos, s_valid):
    return jnp.minimum((jnp.right_shift(qpos, CHUNK_SHIFT) + 1) * CHUNK, s_valid)


def _mla_kernel(q_ref, k_ref, v_ref, o_ref, *, tq, tk, past, s_valid):
    i = pl.program_id(1)
    q0 = past + i * tq
    qpos = q0 + lax.broadcasted_iota(jnp.int32, (tq, 1), 0)
    q_lim = _visible_limit(qpos, s_valid)
    lim_first = _visible_limit(q0, s_valid)
    lim_last = _visible_limit(q0 + tq - 1, s_valid)
    n_full = lim_first // tk
    n_tot = (lim_last + tk - 1) // tk
    q = q_ref[...]
    qhs = [q[:, h * LANES:(h + 1) * LANES] for h in range(A_HEADS)]

    def body(j, carries, masked):
        ks = pl.ds(pl.multiple_of(j * tk, tk), tk)
        if masked:
            kpos = j * tk + lax.broadcasted_iota(jnp.int32, (1, tk), 1)
            vis = kpos < q_lim
        out = []
        for h in range(A_HEADS):
            s = lax.dot_general(qhs[h], k_ref[ks, h * LANES:(h + 1) * LANES], _NT,
                                preferred_element_type=F32)
            if masked:
                s = jnp.where(vis, s, NEG)
            out.append(_online(carries[h], s, v_ref[h, ks, :]))
        return tuple(out)

    init = tuple(_softmax_init(tq, A_V) for _ in range(A_HEADS))
    carries = lax.fori_loop(0, n_full, functools.partial(body, masked=False), init)
    carries = lax.fori_loop(n_full, n_tot, functools.partial(body, masked=True), carries)
    for h in range(A_HEADS):
        _, l, acc = carries[h]
        o_ref[:, h * A_V:(h + 1) * A_V] = acc / l


def _mla(qa, kx, v, tq, tk, past, s_valid):
    b, t, _ = qa.shape
    s_pad = kx.shape[1]
    kern = functools.partial(_mla_kernel, tq=tq, tk=tk, past=past, s_valid=s_valid)
    return pl.pallas_call(
        kern,
        out_shape=jax.ShapeDtypeStruct((b, t, 256), F32),
        grid=(b, t // tq),
        in_specs=[pl.BlockSpec((None, tq, 512), lambda bi, i: (bi, i, 0)),
                  pl.BlockSpec((None, s_pad, 512), lambda bi, i: (bi, 0, 0)),
                  pl.BlockSpec((A_HEADS, None, s_pad, A_V), lambda bi, i: (0, bi, 0, 0))],
        out_specs=pl.BlockSpec((None, tq, 256), lambda bi, i: (bi, i, 0)),
        compiler_params=_cparams(("parallel", "arbitrary")),
        name="mla_attn",
    )(qa, kx, v)


KEY_NONE = -2 ** 31
KEY_MAX = 2 ** 31 - 1


def _order_key(x):
    b = lax.bitcast_convert_type(x, jnp.int32)
    return b ^ (jnp.right_shift(b, 31) & KEY_MAX)


def _order_key_inv(k):
    return lax.bitcast_convert_type(k ^ (jnp.right_shift(k, 31) & KEY_MAX), F32)


def _dsa_kernel(bq_ref, idx_ref, bk_ref, bv_ref, ki_ref, bias_ref, o_ref, sc_ref, sct_ref, msk_ref,
                *, tq, tqs, tkc, past, s_valid, topk):
    i = pl.program_id(1)
    q0 = past + i * tq
    qpos = q0 + lax.broadcasted_iota(jnp.int32, (tq, 1), 0)
    q_lim = _visible_limit(qpos, s_valid)
    lim_last = _visible_limit(q0 + tq - 1, s_valid)
    nch = (lim_last + tkc - 1) // tkc
    ngrp = tkc // LANES
    kf = float(topk)

    def chunk(c):
        return pl.ds(pl.multiple_of(c * tkc, tkc), tkc)

    def key_pos(c):
        return c * tkc + lax.broadcasted_iota(jnp.int32, (1, tkc), 1)

    def key_pos_col(c):
        return (c * tkc + lax.broadcasted_iota(jnp.int32, (tkc, 1), 0)).astype(F32)

    idx = idx_ref[...]
    lane = lax.broadcasted_iota(jnp.int32, (1, LANES), 1)
    qi = idx[:, :LANES]
    qhs = [jnp.where((lane >= h * IDX_DIM) & (lane < (h + 1) * IDX_DIM), qi, 0.0).astype(BF16)
           for h in range(IDX_HEADS)]
    wi = idx[:, LANES + IDX_DIM:LANES + IDX_DIM + IDX_HEADS] * IDX_SCALE
    whs = [wi[:, h:h + 1] for h in range(IDX_HEADS)]

    def score_chunk(c, carry):
        ki = ki_ref[chunk(c), :]
        tot = None
        for h in range(IDX_HEADS):
            d = lax.dot_general(qhs[h], ki, _NT, preferred_element_type=F32)
            term = whs[h] * jnp.maximum(d, 0.0)
            tot = term if tot is None else tot + term
        sc = jnp.where(key_pos(c) < q_lim, _order_key(tot), KEY_NONE)
        sc_ref[:, chunk(c)] = sc
        if tqs > tq:
            sc = jnp.concatenate([sc, jnp.full((tqs - tq, tkc), KEY_NONE, jnp.int32)], axis=0)
        for g in range(ngrp):
            rows = pl.ds(pl.multiple_of(c * tkc + g * LANES, LANES), LANES)
            for rg in range(tqs // LANES):
                sct_ref[rg, rows, :] = sc[rg * LANES:(rg + 1) * LANES, g * LANES:(g + 1) * LANES].T
        return carry

    def keys_t(c):
        return jnp.concatenate([sct_ref[rg, chunk(c), :] for rg in range(tqs // LANES)], axis=1)

    lax.fori_loop(0, nch, score_chunk, 0)

    def fold8(x, op):
        pair = {jnp.sum: jnp.add, jnp.min: jnp.minimum, jnp.max: jnp.maximum}[op]
        parts = [x[8 * r:8 * (r + 1)] for r in range(tkc // 8)]
        while len(parts) > 1:
            odd = parts[len(parts) - 1:] if len(parts) % 2 else []
            parts = [pair(parts[r], parts[r + 1]) for r in range(0, len(parts) - 1, 2)] + odd
        return parts[0]

    def count_where(pred):
        def body(c, acc):
            m = jnp.where(pred(keys_t(c), c), 1.0, 0.0)
            return acc + fold8(m, jnp.sum)
        acc = lax.fori_loop(0, nch, body, jnp.zeros((8, tqs), F32))
        return jnp.sum(acc, axis=0, keepdims=True)

    def count_gt(t):
        return count_where(lambda x, c: x > t)

    def search():
        lane_q = lax.broadcasted_iota(jnp.int32, (1, tqs), 1)
        nvis = jnp.where(lane_q < tq, _visible_limit(q0 + lane_q, s_valid), 0).astype(F32)

        def mm_body(c, carry):
            mn, mx = carry
            x = keys_t(c)
            xm = jnp.where(x == KEY_NONE, KEY_MAX, x)
            return (jnp.minimum(mn, fold8(xm, jnp.min)), jnp.maximum(mx, fold8(x, jnp.max)))
        mn, mx = lax.fori_loop(0, nch, mm_body, (jnp.full((8, tqs), KEY_MAX, jnp.int32),
                                                 jnp.full((8, tqs), KEY_NONE, jnp.int32)))
        mn = jnp.min(mn, axis=0, keepdims=True)
        mx = jnp.max(mx, axis=0, keepdims=True)
        done0 = jnp.where(nvis <= kf, 1.0, 0.0)
        mn = jnp.where(nvis > 0.5, mn, 0)
        mx = jnp.where(nvis > 0.5, mx, 0)
        lo0 = mn - 1
        zeros = jnp.zeros((1, tqs), F32)
        izeros = jnp.zeros((1, tqs), jnp.int32)

        def snap(lo, hi, clo, chi, t, done, tie, vt):
            def body(c, carry):
                a, b = carry
                x = keys_t(c)
                xa = jnp.where(x > lo, x, KEY_MAX)
                xb = jnp.where(x <= hi, x, KEY_NONE)
                return (jnp.minimum(a, fold8(xa, jnp.min)), jnp.maximum(b, fold8(xb, jnp.max)))
            a, b = lax.fori_loop(0, nch, body, (jnp.full((8, tqs), KEY_MAX, jnp.int32),
                                                jnp.full((8, tqs), KEY_NONE, jnp.int32)))
            a = jnp.min(a, axis=0, keepdims=True)
            b = jnp.max(b, axis=0, keepdims=True)
            nd = done < 0.5
            open_ = nd & (a < b)
            b1 = jnp.where(open_, b - 1, hi)
            a0 = jnp.where(open_, a, lo)
            cb = count_gt(b1)
            ca = count_gt(a0)
            b_hit = open_ & (cb == kf)
            b_tie = open_ & (cb > kf)
            rest = open_ & (cb < kf)
            a_hit = rest & (ca == kf)
            a_tie = rest & (ca < kf)
            rest = rest & (ca > kf)
            new_tie = (nd & (a >= b)) | b_tie | a_tie
            t = jnp.where(b_hit, b1, jnp.where(a_hit, a0, t))
            vt = jnp.where(a_tie, a, jnp.where(new_tie, b, vt))
            tie = jnp.where(new_tie, 1.0, tie)
            done = jnp.where(new_tie | b_hit | a_hit, 1.0, done)
            lo = jnp.where(rest, a0, lo)
            clo = jnp.where(rest, ca, clo)
            hi = jnp.where(rest, b1, hi)
            chi = jnp.where(rest, cb, chi)
            return lo, hi, clo, chi, t, done, tie, vt

        def cond(s):
            return (s[1] > 0) & (s[0] < SEARCH_MAX_IT)

        def one_pass(it, s):
            lo, hi, clo, chi, t, done, tie, vt = s
            stuck = hi <= lo + 1
            lo_f, hi_f = _order_key_inv(lo), _order_key_inv(hi)
            itp = lo_f + (hi_f - lo_f) * ((clo - kf) / (clo - chi))
            cand = _order_key(jnp.where(it % 2 == 1, itp, 0.5 * lo_f + 0.5 * hi_f))
            imid = (lo & hi) + jnp.right_shift(lo ^ hi, 1)
            mid = jnp.where((cand > lo) & (cand < hi), cand, imid)
            c = count_gt(mid)
            nd = done < 0.5
            live = nd & jnp.logical_not(stuck)
            hit = live & (c == kf)
            stk = nd & stuck
            up = live & (c > kf)
            dn = live & (c < kf)
            t = jnp.where(hit, mid, t)
            vt = jnp.where(stk, hi, vt)
            tie = jnp.where(stk, 1.0, tie)
            lo = jnp.where(up, mid, lo)
            clo = jnp.where(up, c, clo)
            hi = jnp.where(dn, mid, hi)
            chi = jnp.where(dn, c, chi)
            done = jnp.where(hit | stk, 1.0, done)
            return lo, hi, clo, chi, t, done, tie, vt

        def body(s):
            it = s[0] + 2
            st = one_pass(it - 1, one_pass(it - 2, s[2:]))
            do_snap = (it >= SNAP_START) & ((it - SNAP_START) % SNAP_EVERY == 0)
            lo, hi, clo, chi, t, done, tie, vt = lax.cond(do_snap, snap, lambda *a: a, *st)
            active = (jnp.max(1.0 - done) > 0.5).astype(jnp.int32)
            return it, active, lo, hi, clo, chi, t, done, tie, vt

        active0 = (jnp.max(1.0 - done0) > 0.5).astype(jnp.int32)
        s = lax.while_loop(cond, body, (jnp.int32(0), active0, lo0, mx, nvis, zeros, lo0, done0,
                                        zeros, izeros))
        t, tie, vt = s[6], s[8], s[9]

        def tie_cut():
            t2 = jnp.where(tie > 0.5, vt, t)
            need = kf - count_gt(t2)

            def bs_body(_, carry):
                lo_c, hi_c = carry
                mid_c = jnp.floor((lo_c + hi_c) * 0.5)
                cnt = count_where(lambda x, c: (x == vt) & (key_pos_col(c) < mid_c))
                ok = cnt >= need
                return jnp.where(ok, lo_c, mid_c), jnp.where(ok, mid_c, hi_c)

            nbits = int(s_valid).bit_length()
            _, cut = lax.fori_loop(0, nbits, bs_body,
                                   (zeros, jnp.full((1, tqs), float(1 << nbits), F32)))
            return t2, jnp.where(tie > 0.5, cut, 0.0)

        any_tie = jnp.max(tie) > 0.5
        t, cut = lax.cond(any_tie, tie_cut, lambda: (t, zeros))
        return t, tie, vt, cut

    def no_search():
        z = jnp.zeros((1, tqs), F32)
        return (jnp.full((1, tqs), KEY_NONE, jnp.int32), z, jnp.zeros((1, tqs), jnp.int32), z)

    t, tie, vt, cut = lax.cond(lim_last > topk, search, no_search)

    def to_col(v):
        groups = [jnp.broadcast_to(v[:, rg * LANES:(rg + 1) * LANES], (LANES, LANES)).T
                  for rg in range(tqs // LANES)]
        full = groups[0] if len(groups) == 1 else jnp.concatenate(groups, axis=0)
        return full[:tq]

    t, vt, cut = to_col(t), to_col(vt), to_col(cut)
    tied = to_col(tie) > 0.5
    lane_f = lax.broadcasted_iota(jnp.int32, (1, LANES), 1).astype(F32)

    def to_mask(c, carry):
        for g in range(ngrp):
            cols = pl.ds(pl.multiple_of(c * tkc + g * LANES, LANES), LANES)
            x = sc_ref[:, cols]
            kpos = (c * tkc + g * LANES).astype(F32) + lane_f
            sel = (x > t) | (tied & (x == vt) & (kpos < cut))
            msk_ref[:, cols] = jnp.where(sel, 0.0, NEG)
        return carry

    lax.fori_loop(0, nch, to_mask, 0)

    bq = (bq_ref[...] * (B_SCALE * LOG2E)).astype(BF16)
    qbs = [bq[:, h * B_DIM:(h + 1) * B_DIM] for h in range(B_HEADS)]
    d0 = q0 // LANES

    def att_step(c, carries, nw):
        ks = pl.ds(pl.multiple_of(c * tkc, tkc), nw * tkc)
        msk = msk_ref[:, ks]
        rows = min(tq, LANES)
        out = []
        for h in range(B_HEADS):
            s = lax.dot_general(qbs[h], bk_ref[h, ks, :], _NT, preferred_element_type=F32)
            bias = jnp.concatenate(
                [jnp.concatenate([bias_ref[jnp.clip(d0 + rg - (c * ngrp + g), 0, 2), h, :rows, :]
                                  for g in range(nw * ngrp)], axis=1)
                 for rg in range(-(-tq // LANES))], axis=0)
            out.append(_online(carries[h], s + (msk + bias), bv_ref[h, ks, :]))
        return tuple(out)

    init = tuple(_softmax_init(tq, B_DIM) for _ in range(B_HEADS))
    nquad = nch // 4
    npair = (nch - 4 * nquad) // 2
    carries = lax.fori_loop(0, nquad, lambda j, cs: att_step(4 * j, cs, 4), init)
    carries = lax.fori_loop(0, npair, lambda j, cs: att_step(4 * nquad + 2 * j, cs, 2), carries)
    carries = lax.fori_loop(4 * nquad + 2 * npair, nch, lambda c, cs: att_step(c, cs, 1), carries)
    for h in range(B_HEADS):
        _, l, acc = carries[h]
        o_ref[:, h * B_DIM:(h + 1) * B_DIM] = acc / l


def _dsa(proj3, bk, bv, ki, bias, tq, tkc, past, s_valid, topk):
    b, t, _ = proj3.shape
    s_pad = bk.shape[2]
    tqs = -(-tq // LANES) * LANES
    assert past % LANES == 0 and (tq % LANES == 0 or t == tq) and s_pad % tkc == 0
    kern = functools.partial(_dsa_kernel, tq=tq, tqs=tqs, tkc=tkc, past=past, s_valid=s_valid,
                             topk=topk)
    return pl.pallas_call(
        kern,
        out_shape=jax.ShapeDtypeStruct((b, t, 256), F32),
        grid=(b, t // tq),
        in_specs=[pl.BlockSpec((None, tq, 256), lambda bi, i: (bi, i, CB_BQ)),
                  pl.BlockSpec((None, tq, 256), lambda bi, i: (bi, i, CB_IDX)),
                  pl.BlockSpec((None, B_HEADS, s_pad, B_DIM), lambda bi, i: (bi, 0, 0, 0)),
                  pl.BlockSpec((None, B_HEADS, s_pad, B_DIM), lambda bi, i: (bi, 0, 0, 0)),
                  pl.BlockSpec((None, s_pad, LANES), lambda bi, i: (bi, 0, 0)),
                  pl.BlockSpec((3, B_HEADS, LANES, LANES), lambda bi, i: (0, 0, 0, 0))],
        out_specs=pl.BlockSpec((None, tq, 256), lambda bi, i: (bi, i, 0)),
        scratch_shapes=[pltpu.VMEM((tq, s_pad), jnp.int32), pltpu.VMEM((tqs // LANES, s_pad, LANES), jnp.int32),
                        pltpu.VMEM((tq, s_pad), F32)],
        compiler_params=_cparams(("parallel", "arbitrary")),
        name="dsa_attn",
    )(proj3, proj3, bk, bv, ki, bias)


def _pool_kernel(u_ref, prev_ref, w_ref, scale_ref, o_ref, st_ref, ext_ref, *, tr, past):
    i = pl.program_id(1)

    @pl.when(i == 0)
    def _():
        ext_ref[1:16, :] = prev_ref[...]

    u = u_ref[...]
    ext_ref[16:16 + tr, :] = u

    def tap(k, lo):
        return ext_ref[16 - k:16 - k + tr, lo:lo + LANES]

    pos1 = past + i * tr + 1 + lax.broadcasted_iota(jnp.int32, (tr, 1), 0)
    cnt = [jnp.minimum(w, pos1).astype(F32) for w in POOL_WINDOWS]
    lane = lax.broadcasted_iota(jnp.int32, (1, LANES), 1)
    first = lane < POOL_GROUP

    s2 = tap(0, 0) + tap(1, 0)
    s4 = s2 + tap(2, 0) + tap(3, 0)
    s8 = tap(0, LANES)
    for k in range(1, 8):
        s8 = s8 + tap(k, LANES)
    s16 = s8
    for k in range(8, 16):
        s16 = s16 + tap(k, LANES)
    m_lo = jnp.where(first, s2 / cnt[0], s4 / cnt[1])
    m_hi = jnp.where(first, s8 / cnt[2], s16 / cnt[3])
    pooled = jnp.concatenate([m_lo, m_hi], axis=1) - u
    mixed = jnp.dot(pooled.astype(BF16), w_ref[...], preferred_element_type=F32)
    o_ref[...] = mixed * scale_ref[...]

    new_state = ext_ref[1 + tr:16 + tr, :]
    st_ref[...] = new_state
    ext_ref[1:16, :] = new_state


def _pool(proj3, prev, w_bd, scale, tr, past):
    b, t, _ = proj3.shape
    kern = functools.partial(_pool_kernel, tr=tr, past=past)
    return pl.pallas_call(
        kern,
        out_shape=(jax.ShapeDtypeStruct((b, t, POOL_WIDTH), F32),
                   jax.ShapeDtypeStruct((b, POOL_STATE, POOL_WIDTH), F32)),
        grid=(b, t // tr),
        in_specs=[pl.BlockSpec((None, tr, 256), lambda bi, i: (bi, i, CB_POOL)),
                  pl.BlockSpec((None, POOL_STATE, POOL_WIDTH), lambda bi, i: (bi, 0, 0)),
                  pl.BlockSpec((256, 256), lambda bi, i: (0, 0)),
                  pl.BlockSpec((1, 256), lambda bi, i: (0, 0))],
        out_specs=(pl.BlockSpec((None, tr, 256), lambda bi, i: (bi, i, 0)),
                   pl.BlockSpec((None, POOL_STATE, POOL_WIDTH), lambda bi, i: (bi, 0, 0))),
        scratch_shapes=[pltpu.VMEM((tr + 16, POOL_WIDTH), F32)],
        compiler_params=_cparams(("parallel", "arbitrary")),
        name="pool_mix",
    )(proj3, prev, w_bd, scale)


def _conv_kernel(u_ref, prev_ref, w_ref, b_ref, g_ref, beta_ref, o_ref, st_ref, ext_ref, sh_ref,
                 *, tr, sub):
    i = pl.program_id(1)
    npre = CONV_K - 1

    @pl.when(i == 0)
    def _():
        ext_ref[2:2 + npre, :] = prev_ref[...]

    u = u_ref[...]
    ext_ref[32:32 + tr, :] = u[:, :CONV_WIDTH] * _sigmoid(u[:, CONV_WIDTH:])

    for r in range(8):
        n = tr + 8 * ((CONV_K - 1 - r) // 8)
        sh_ref[r, 0:n, :] = ext_ref[2 + r:2 + r + n, :]

    for r0 in range(0, tr, sub):
        acc = None
        for k in range(CONV_K):
            a, r = divmod(k, 8)
            term = sh_ref[r, r0 + 8 * a:r0 + 8 * a + sub, :] * w_ref[k:k + 1, :]
            acc = term if acc is None else acc + term
        y = _ln_rows(acc + b_ref[...], g_ref[...], beta_ref[...])
        o_ref[r0:r0 + sub, :] = y * _sigmoid(y)

    new_state = ext_ref[2 + tr:2 + tr + npre, :]
    st_ref[...] = new_state
    ext_ref[2:2 + npre, :] = new_state


def _conv(proj3, prev, w, bias, g, beta, tr, sub):
    b, t, _ = proj3.shape
    npre = CONV_K - 1
    kern = functools.partial(_conv_kernel, tr=tr, sub=sub)
    vec = pl.BlockSpec((1, CONV_WIDTH), lambda bi, i: (0, 0))
    return pl.pallas_call(
        kern,
        out_shape=(jax.ShapeDtypeStruct((b, t, CONV_WIDTH), F32),
                   jax.ShapeDtypeStruct((b, npre, CONV_WIDTH), F32)),
        grid=(b, t // tr),
        in_specs=[pl.BlockSpec((None, tr, 512), lambda bi, i: (bi, i, CB_CONV)),
                  pl.BlockSpec((None, npre, CONV_WIDTH), lambda bi, i: (bi, 0, 0)),
                  pl.BlockSpec((CONV_K, CONV_WIDTH), lambda bi, i: (0, 0)),
                  vec, vec, vec],
        out_specs=(pl.BlockSpec((None, tr, CONV_WIDTH), lambda bi, i: (bi, i, 0)),
                   pl.BlockSpec((None, npre, CONV_WIDTH), lambda bi, i: (bi, 0, 0))),
        scratch_shapes=[pltpu.VMEM((tr + 32, CONV_WIDTH), F32),
                        pltpu.VMEM((8, tr + 24, CONV_WIDTH), F32)],
        compiler_params=_cparams(("parallel", "arbitrary")),
        name="conv_module",
    )(proj3, prev, w, bias, g, beta)


def _merge_kernel(a_ref, b_ref, c_ref, d_ref, x_ref, wg_ref, wb_ref, wo_ref, g_ref, beta_ref,
                  o_ref):
    x = x_ref[...]
    xb = x.astype(BF16)
    mixed = None
    for n, br_ref in enumerate((a_ref, b_ref, c_ref, d_ref)):
        gate = jnp.dot(xb, wg_ref[:, n * D_MODEL:(n + 1) * D_MODEL], preferred_element_type=F32)
        br = jnp.dot(br_ref[...].astype(BF16), wb_ref[n], preferred_element_type=F32)
        term = _sigmoid(gate) * br
        mixed = term if mixed is None else mixed + term
    y = jnp.dot(mixed.astype(BF16), wo_ref[...], preferred_element_type=F32)
    o_ref[...] = _ln_rows(ALPHA * x + y, g_ref[...], beta_ref[...])


def _merge(oa, ob, oc, od, x3, wg, wb, wo, g, beta, tm):
    b, t, _ = x3.shape
    br_spec = pl.BlockSpec((None, tm, 256), lambda bi, i: (bi, i, 0))
    const = lambda shape: pl.BlockSpec(shape, lambda bi, i: (0,) * len(shape),
                                       pipeline_mode=pl.Buffered(1))
    return pl.pallas_call(
        _merge_kernel,
        out_shape=jax.ShapeDtypeStruct((b, t, D_MODEL), F32),
        grid=(b, t // tm),
        in_specs=[br_spec, br_spec, br_spec, br_spec,
                  pl.BlockSpec((None, tm, D_MODEL), lambda bi, i: (bi, i, 0)),
                  const((D_MODEL, N_BRANCH * D_MODEL)), const((N_BRANCH, 256, D_MODEL)),
                  const((D_MODEL, D_MODEL)), const((1, D_MODEL)), const((1, D_MODEL))],
        out_specs=pl.BlockSpec((None, tm, D_MODEL), lambda bi, i: (bi, i, 0)),
        compiler_params=_cparams(("parallel", "parallel")),
        name="merge_ln1",
    )(oa, ob, oc, od, x3, wg, wb, wo, g, beta)


FF_TC = 256
FF_NC = D_FF // FF_TC


def _ffn_kernel(x_ref, st_ref, wup_ref, cw_ref, cb_ref, wd_ref, g_ref, beta_ref, o_ref, nst_ref,
                car_ref, ubuf_ref, h_ref, *, tm):
    @pl.when(pl.program_id(1) == 0)
    def _():
        car_ref[6:8, :] = st_ref[...]

    x = x_ref[...]
    xb = x.astype(BF16)

    def conv3(off, slot):
        cols = slice(off, off + FF_TC)
        u = jnp.dot(xb, wup_ref[:, cols], preferred_element_type=F32)
        ubuf_ref[slot, 6:8, :] = car_ref[6:8, cols]
        ubuf_ref[slot, 8:8 + tm, :] = u
        y = (cw_ref[0:1, cols] * ubuf_ref[slot, 6:6 + tm, :]
             + cw_ref[1:2, cols] * ubuf_ref[slot, 7:7 + tm, :]
             + cw_ref[2:3, cols] * u + cb_ref[:, cols])
        car_ref[6:8, cols] = ubuf_ref[slot, 6 + tm:8 + tm, :]
        return y

    for c in range(FF_NC):
        val = conv3(c * FF_TC, (2 * c) % 4)
        gate = conv3(D_FF + c * FF_TC, (2 * c + 1) % 4)
        h_ref[:, c * FF_TC:(c + 1) * FF_TC] = ((gate * _sigmoid(gate)) * val).astype(BF16)

    nst_ref[...] = car_ref[6:8, :]
    y = jnp.dot(h_ref[...], wd_ref[...], preferred_element_type=F32)
    o_ref[...] = _ln_rows(ALPHA * x + y, g_ref[...], beta_ref[...])


def _ffn(x3, st, w_up, cw, cb, w_dn, g, beta, tm):
    b, t, _ = x3.shape
    kern = functools.partial(_ffn_kernel, tm=tm)
    nk = FFN_K - 1
    const = lambda shape: pl.BlockSpec(shape, lambda bi, i: (0,) * len(shape),
                                       pipeline_mode=pl.Buffered(1))
    return pl.pallas_call(
        kern,
        out_shape=(jax.ShapeDtypeStruct((b, t, D_MODEL), F32),
                   jax.ShapeDtypeStruct((b, nk, 2 * D_FF), F32)),
        grid=(b, t // tm),
        in_specs=[pl.BlockSpec((None, tm, D_MODEL), lambda bi, i: (bi, i, 0)),
                  pl.BlockSpec((None, nk, 2 * D_FF), lambda bi, i: (bi, 0, 0)),
                  const((D_MODEL, 2 * D_FF)), const((FFN_K, 2 * D_FF)), const((1, 2 * D_FF)),
                  const((D_FF, D_MODEL)), const((1, D_MODEL)), const((1, D_MODEL))],
        out_specs=(pl.BlockSpec((None, tm, D_MODEL), lambda bi, i: (bi, i, 0)),
                   pl.BlockSpec((None, nk, 2 * D_FF), lambda bi, i: (bi, 0, 0))),
        scratch_shapes=[pltpu.VMEM((8, 2 * D_FF), F32),
                        pltpu.VMEM((4, tm + 8, FF_TC), F32),
                        pltpu.VMEM((tm, D_FF), BF16)],
        compiler_params=_cparams(("parallel", "arbitrary")),
        name="conv_ffn_ln2",
    )(x3, st, w_up, cw, cb, w_dn, g, beta)


def _rope_tables(past, t):
    half = A_ROPE // 2
    freqs = ROPE_BASE ** (-jnp.arange(half, dtype=F32) / half)
    pos = past + jnp.arange(t, dtype=jnp.int32)
    ang = pos.astype(F32)[:, None] * freqs[None, :]
    cos, sin = jnp.cos(ang), jnp.sin(ang)
    c2 = jnp.concatenate([cos, cos], axis=1)
    s2 = jnp.concatenate([sin, sin], axis=1)
    one = jnp.ones((t, A_NOPE), F32)
    z = lambda n: jnp.zeros((t, n), F32)
    cosq = jnp.concatenate([one, c2, z(LANES - A_NOPE - A_ROPE)], axis=1)
    sinq = jnp.concatenate([z(A_NOPE), s2, z(LANES - A_NOPE - A_ROPE)], axis=1)
    cosk = jnp.concatenate([c2, z(LANES - A_ROPE)], axis=1)
    sink = jnp.concatenate([s2, z(LANES - A_ROPE)], axis=1)
    return cosq, sinq, cosk, sink


_BUCKET_EDGES = (12, 16, 23, 32, 46, 64, 91)


def _rel_bucket(rel):
    nb = REL_BUCKETS // 2
    max_exact = nb // 2
    n = np.abs(rel)
    large = max_exact + sum((n >= e).astype(np.int64) for e in _BUCKET_EDGES)
    return np.where(rel > 0, nb, 0) + np.where(n < max_exact, n, large)


def _bias_tiles(rel_bias):
    r = np.arange(LANES)[:, None]
    c = np.arange(LANES)[None, :]
    buckets = np.stack([_rel_bucket(c - r - LANES * d) for d in range(3)])
    tiles = jnp.zeros((B_HEADS,) + buckets.shape, F32)
    for bkt in np.unique(buckets):
        sel = jnp.asarray(buckets == bkt)
        tiles = jnp.where(sel[None], rel_bias[int(bkt)][:, None, None, None], tiles)
    return jnp.transpose(tiles, (1, 0, 2, 3)) * LOG2E


def _pack_w_in(w):
    z = lambda n: jnp.zeros((D_MODEL, n), F32)
    kr = w[:, O_KROPE:O_KROPE + A_ROPE]
    half = A_ROPE // 2
    kr_rot = jnp.concatenate([-kr[:, half:], kr[:, :half]], axis=1)
    parts = [
        w[:, O_CONV:O_CONV + 2 * CONV_WIDTH],
        w[:, O_CQ:O_CQ + A_Q_LORA], z(256 - A_Q_LORA),
        w[:, O_CKV:O_CKV + A_KV_LORA], kr, kr_rot, z(256 - A_KV_LORA - 2 * A_ROPE),
        w[:, O_BQ:O_BQ + 256], w[:, O_BK:O_BK + 256], w[:, O_BV:O_BV + 256],
        w[:, O_QIDX:O_QIDX + 128], w[:, O_KIDX:O_KIDX + IDX_DIM], w[:, O_WIDX:O_WIDX + IDX_HEADS],
        z(256 - 128 - IDX_DIM - IDX_HEADS),
        w[:, O_POOL:O_POOL + POOL_WIDTH],
    ]
    return jnp.concatenate(parts, axis=1).astype(BF16)


def _pack_w_qup(w):
    zrow = lambda m: jnp.zeros((256 - A_Q_LORA, m.shape[1]), F32)
    half = A_ROPE // 2
    dq = A_NOPE + A_ROPE
    cols, rcols = [], []
    for h in range(A_HEADS):
        nope = w[:, h * dq:h * dq + A_NOPE]
        rp = w[:, h * dq + A_NOPE:(h + 1) * dq]
        pad = jnp.zeros((A_Q_LORA, LANES - dq), F32)
        cols += [nope, rp, pad]
        rcols += [jnp.zeros_like(nope), -rp[:, half:], rp[:, :half], pad]
    wq = jnp.concatenate(cols, axis=1)
    wqr = jnp.concatenate(rcols, axis=1)
    wq = jnp.concatenate([wq, zrow(wq)], axis=0).astype(BF16)
    wqr = jnp.concatenate([wqr, zrow(wqr)], axis=0).astype(BF16)
    return wq, wqr


def _pack_w_kvup(w):
    dk = A_NOPE + A_V
    kc, vc = [], []
    for h in range(A_HEADS):
        kc += [w[:, h * dk:h * dk + A_NOPE], jnp.zeros((A_KV_LORA, LANES - A_NOPE), F32)]
        vc += [w[:, h * dk + A_NOPE:(h + 1) * dk]]
    return jnp.concatenate(kc, axis=1).astype(BF16), jnp.concatenate(vc, axis=1).astype(BF16)


def _rope_placement():
    e = np.zeros((LANES, A_HEADS * LANES), np.float32)
    for h in range(A_HEADS):
        for r in range(A_ROPE):
            e[r, h * LANES + A_NOPE + r] = 1.0
    return jnp.asarray(e, dtype=BF16)


def _pool_blockdiag(pw):
    out = jnp.zeros((POOL_WIDTH, POOL_WIDTH), F32)
    for g in range(len(POOL_WINDOWS)):
        out = out.at[g * POOL_GROUP:(g + 1) * POOL_GROUP, g * POOL_GROUP:(g + 1) * POOL_GROUP].set(pw[g])
    return out.astype(BF16)


class _Tiles:
    def __init__(self, t):
        prompt = t > 128
        self.rows = 1024 if prompt else 128
        self.in_tn = 1152
        self.prep = 1024 if prompt else t
        self.kv = 2048 if prompt else 17 * LANES
        self.mla_q = 1024 if prompt else t
        self.mla_k = 1024 if prompt else 17 * LANES
        self.dsa_q = 256 if prompt else t
        self.dsa_kc = 512 if prompt else 17 * LANES
        self.pool = 1024 if prompt else t
        self.conv = 512 if prompt else t
        self.conv_sub = 64 if prompt else t
        self.merge = 512 if prompt else t
        self.ffn = 512 if prompt else t
        self.s_align = 512 if prompt else 17 * LANES


def _trunk_layer(x3, caches, lw, consts, past, stacks, layer):
    (w_in, w_gates, gq, gkv, wq, wqr, wkn, wv, pool_bd, pool_scale, conv_w, conv_b, conv_g, conv_beta,
     w_branch, w_out, ln1_g, ln1_b, w_up, ffn_cw, ffn_cb, w_down, ln2_g, ln2_b) = lw
    tabs, bias, e_place = consts
    b, t, _ = x3.shape
    tl = _Tiles(t)
    s_valid = past + t
    s_pad = -(-s_valid // tl.s_align) * tl.s_align
    topk = min(TOPK_MAX, s_valid // 4)

    proj = _inproj(x3.reshape(b * t, D_MODEL), w_in, tl.rows, tl.in_tn)
    proj3 = proj.reshape(b, t, PROJ_W)
    (qa, kr128, bk_hm, bv_hm, ki_rep), stacks = _aprep(
        proj3, tabs, gq, gkv, wq, wqr, stacks, layer, tl.prep)
    ckv_n = stacks[0][layer]

    def keys(cache, new, width):
        parts = [new] if cache is None else [cache.reshape(b, past, -1), new]
        if s_pad > s_valid:
            parts.append(jnp.zeros((b, s_pad - s_valid, width), new.dtype))
        return parts[0] if len(parts) == 1 else jnp.concatenate(parts, axis=1)

    c_ckv, c_krope, c_bk, c_bv, c_kidx, st_pool, st_conv, st_ffn = caches
    ckv_all = keys(c_ckv, ckv_n, A_KV_LORA)
    if c_krope is None:
        kr_all = keys(None, kr128, LANES)
    else:
        c_kr128 = jnp.pad(c_krope, ((0, 0), (0, 0), (0, LANES - A_ROPE)))
        kr_all = keys(c_kr128, kr128, LANES)
    if c_bk is None and s_pad == s_valid:
        bk_all, bv_all, ki_all = bk_hm, bv_hm, ki_rep
    else:
        head_major = lambda a: jnp.transpose(a.astype(BF16).reshape(b, s_pad, B_HEADS, B_DIM),
                                             (0, 2, 1, 3))
        bk_all = head_major(keys(c_bk, stacks[2][layer], 256))
        bv_all = head_major(keys(c_bv, stacks[3][layer], 256))
        ki_all = jnp.tile(keys(c_kidx, stacks[4][layer], IDX_DIM).astype(BF16), (1, 1, IDX_HEADS))

    kx, vx = _kvup(ckv_all.reshape(b * s_pad, A_KV_LORA), kr_all.reshape(b * s_pad, LANES),
                   wkn, wv, e_place, tl.kv)
    out_a = _mla(qa, kx.reshape(b, s_pad, 512), vx.reshape(A_HEADS, b, s_pad, A_V),
                 tl.mla_q, tl.mla_k, past, s_valid)
    out_b = _dsa(proj3, bk_all, bv_all, ki_all, bias, tl.dsa_q, tl.dsa_kc, past, s_valid, topk)
    out_c, new_pool = _pool(proj3, st_pool, pool_bd, pool_scale, tl.pool, past)
    out_d, new_conv = _conv(proj3, st_conv, conv_w, conv_b, conv_g, conv_beta, tl.conv, tl.conv_sub)
    x1 = _merge(out_a, out_b, out_c, out_d, x3, w_gates, w_branch, w_out, ln1_g, ln1_b, tl.merge)
    x2, new_ffn = _ffn(x1, st_ffn, w_up, ffn_cw, ffn_cb, w_down, ln2_g, ln2_b, tl.ffn)
    return x2, stacks, (new_pool, new_conv, new_ffn)


def kernel(x_prompt, x_sample, cache_a_ckv, cache_a_krope, cache_b_k, cache_b_v, cache_b_kidx,
           state_pool, state_conv, state_ffn, rel_bias, ln_in_g, ln_in_b, w_in, a_q_norm, a_kv_norm,
           a_w_qup, a_w_kvup, pool_w, pool_scale, conv_w, conv_b, conv_ln_g, conv_ln_b, w_branch,
           w_out, ln1_g, ln1_b, w_up, ffn_conv_w, ffn_conv_b, w_down, ln2_g, ln2_b):
    bp, tp, _ = x_prompt.shape
    bs, ts, _ = x_sample.shape
    past = cache_a_ckv.shape[2]
    depth = w_in.shape[0]

    xp = _layer_norm(x_prompt.reshape(bp * tp, D_MODEL), ln_in_g, ln_in_b,
                     _Tiles(tp).rows).reshape(bp, tp, D_MODEL)
    xs = _layer_norm(x_sample.reshape(bs * ts, D_MODEL), ln_in_g, ln_in_b,
                     _Tiles(ts).rows).reshape(bs, ts, D_MODEL)

    e_place = _rope_placement()
    bias = _bias_tiles(rel_bias)
    consts_p = (_rope_tables(0, tp), bias, e_place)
    consts_s = (_rope_tables(past, ts), bias, e_place)
    zeros_p = (None, None, None, None, None,
               jnp.zeros((bp, POOL_STATE, POOL_WIDTH), F32),
               jnp.zeros((bp, CONV_K - 1, CONV_WIDTH), F32),
               jnp.zeros((bp, FFN_K - 1, 2 * D_FF), F32))

    def state_stacks(b, t):
        return tuple(jnp.zeros((depth, b, t, w), F32)
                     for w in (A_KV_LORA, A_ROPE, 256, 256, IDX_DIM))

    stacks_p, stacks_s = state_stacks(bp, tp), state_stacks(bs, ts)
    row = lambda v: v.reshape(1, -1)
    p_states, s_states = [], []
    for l in range(depth):
        wq, wqr = _pack_w_qup(a_w_qup[l])
        wkn, wv = _pack_w_kvup(a_w_kvup[l])
        gq = jnp.concatenate([a_q_norm[l], jnp.zeros((256 - A_Q_LORA,), F32)]).reshape(1, 256)
        w_gates = w_in[l][:, O_GATES:O_GATES + N_BRANCH * D_MODEL].astype(BF16)
        lw = (_pack_w_in(w_in[l]), w_gates, gq, row(a_kv_norm[l]), wq, wqr, wkn, wv,
              _pool_blockdiag(pool_w[l]), row(pool_scale[l]), conv_w[l], row(conv_b[l]),
              row(conv_ln_g[l]), row(conv_ln_b[l]), w_branch[l].astype(BF16), w_out[l].astype(BF16),
              row(ln1_g[l]), row(ln1_b[l]), w_up[l].astype(BF16), ffn_conv_w[l], row(ffn_conv_b[l]),
              w_down[l].astype(BF16), row(ln2_g[l]), row(ln2_b[l]))
        xp, stacks_p, st_p = _trunk_layer(xp, zeros_p, lw, consts_p, 0, stacks_p, l)
        caches_s = (cache_a_ckv[l], cache_a_krope[l], cache_b_k[l], cache_b_v[l], cache_b_kidx[l],
                    state_pool[l], state_conv[l], state_ffn[l])
        xs, stacks_s, st_s = _trunk_layer(xs, caches_s, lw, consts_s, past, stacks_s, l)
        p_states.append(st_p)
        s_states.append(st_s)

    def outputs(stacks, small, b, t):
        ckv, krope, bk, bv, kidx = stacks
        heads = lambda a: a.reshape(depth, b, t, B_HEADS, B_DIM)
        return [ckv, krope, heads(bk), heads(bv), kidx] + [jnp.stack(a) for a in zip(*small)]

    return (xp, xs, *outputs(stacks_p, p_states, bp, tp), *outputs(stacks_s, s_states, bs, ts))
```

```python
import functools

import numpy as np
import jax
import jax.numpy as jnp
from jax import lax
from jax.experimental import pallas as pl
from jax.experimental.pallas import tpu as pltpu

F32 = jnp.float32
BF16 = jnp.bfloat16

D_MODEL = 1024
DEPTH = 4
CHUNK = 64
CHUNK_SHIFT = 6
N_BRANCH = 4
A_HEADS = 4
A_NOPE = 64
A_ROPE = 32
A_V = 64
A_Q_LORA = 192
A_KV_LORA = 128
A_SCALE = (A_NOPE + A_ROPE) ** -0.5
ROPE_BASE = 10000.0
B_HEADS = 4
B_DIM = 64
B_SCALE = B_DIM ** -0.5
IDX_HEADS = 4
IDX_DIM = 32
IDX_SCALE = (IDX_HEADS ** -0.5) * (IDX_DIM ** -0.5)
TOPK_MAX = 256
REL_BUCKETS = 32
POOL_WINDOWS = (2, 4, 8, 16)
POOL_WIDTH = 256
POOL_GROUP = 64
POOL_STATE = 15
CONV_WIDTH = 256
CONV_K = 31
D_FF = 2816
FFN_K = 3
ALPHA = (2 * DEPTH) ** 0.25
LN_EPS = 1e-5

IN_SPLITS = (A_Q_LORA, A_KV_LORA, A_ROPE, 256, 256, 256, 128, IDX_DIM, IDX_HEADS,
             POOL_WIDTH, 2 * CONV_WIDTH, N_BRANCH * D_MODEL)
_OFF = [0] + [int(v) for v in np.cumsum(IN_SPLITS)]
(O_CQ, O_CKV, O_KROPE, O_BQ, O_BK, O_BV, O_QIDX, O_KIDX, O_WIDX, O_POOL, O_CONV, O_GATES,
 D_IN) = _OFF

PROJ_W = 2304
CB_CONV = 0
CB_CQ = 2
CB_CK = 3
CB_BQ = 4
CB_BK = 5
CB_BV = 6
CB_IDX = 7
CB_POOL = 8

VMEM_LIMIT_BYTES = 48 * 1024 * 1024
LANES = 128

LOG2E = 1.4426950408889634
NEG = -1e30
M_INIT = -5e29
SEARCH_MAX_IT = 320
SNAP_START = 18
SNAP_EVERY = 2

_NT = (((1,), (1,)), ((), ()))


def _cparams(sem):
    return pltpu.CompilerParams(dimension_semantics=sem, vmem_limit_bytes=VMEM_LIMIT_BYTES)


def _ln_rows(x, g, b):
    mu = jnp.mean(x, axis=-1, keepdims=True)
    xc = x - mu
    var = jnp.mean(xc * xc, axis=-1, keepdims=True)
    return xc * lax.rsqrt(var + LN_EPS) * g + b


def _sigmoid(x):
    return 1.0 / (1.0 + jnp.exp(-x))


def _ln_kernel(x_ref, g_ref, b_ref, o_ref):
    o_ref[...] = _ln_rows(x_ref[...], g_ref[...], b_ref[...])


def _layer_norm(x2d, g, b, tm):
    n, d = x2d.shape
    return pl.pallas_call(
        _ln_kernel,
        out_shape=jax.ShapeDtypeStruct((n, d), F32),
        grid=(n // tm,),
        in_specs=[pl.BlockSpec((tm, d), lambda i: (i, 0)),
                  pl.BlockSpec((1, d), lambda i: (0, 0)),
                  pl.BlockSpec((1, d), lambda i: (0, 0))],
        out_specs=pl.BlockSpec((tm, d), lambda i: (i, 0)),
        compiler_params=_cparams(("parallel",)),
        name="input_ln",
    )(x2d, g.reshape(1, d), b.reshape(1, d))


def _inproj_kernel(x_ref, w_ref, o_ref, xb_ref):
    @pl.when(pl.program_id(1) == 0)
    def _():
        xb_ref[...] = x_ref[...].astype(BF16)

    o_ref[...] = jnp.dot(xb_ref[...], w_ref[...], preferred_element_type=F32)


def _inproj(x2d, w, tm, tn):
    n, k = x2d.shape
    nw = w.shape[1]
    return pl.pallas_call(
        _inproj_kernel,
        out_shape=jax.ShapeDtypeStruct((n, nw), F32),
        grid=(n // tm, nw // tn),
        in_specs=[pl.BlockSpec((tm, k), lambda i, j: (i, 0)),
                  pl.BlockSpec((k, tn), lambda i, j: (0, j))],
        out_specs=pl.BlockSpec((tm, tn), lambda i, j: (i, j)),
        scratch_shapes=[pltpu.VMEM((tm, k), BF16)],
        compiler_params=_cparams(("parallel", "arbitrary")),
        name="in_proj",
    )(x2d, w)


def _aprep_kernel(cq_ref, ck_ref, bk_ref, bv_ref, idx_ref, cosq_ref, sinq_ref, cosk_ref, sink_ref,
                  gq_ref, gkv_ref, wq_ref, wqr_ref, _ckv_in, _kro_in, _sbk_in, _sbv_in, _ski_in,
                  qa_ref, kr_ref, hbk_ref, hbv_ref, rki_ref,
                  ckv_ref, kro_ref, sbk_ref, sbv_ref, ski_ref):
    bk = bk_ref[...]
    bv = bv_ref[...]
    sbk_ref[...] = bk
    sbv_ref[...] = bv
    bkb = bk.astype(BF16)
    bvb = bv.astype(BF16)
    for h in range(B_HEADS):
        hbk_ref[h] = bkb[:, h * B_DIM:(h + 1) * B_DIM]
        hbv_ref[h] = bvb[:, h * B_DIM:(h + 1) * B_DIM]
    ki = idx_ref[:, LANES:]
    ski_ref[...] = ki[:, :IDX_DIM]
    lane = lax.broadcasted_iota(jnp.int32, (1, LANES), 1)
    k0 = jnp.where(lane < IDX_DIM, ki, 0.0)
    rep = k0
    for h in range(1, IDX_HEADS):
        rep = rep + pltpu.roll(k0, h * IDX_DIM, 1)
    rki_ref[...] = rep.astype(BF16)

    cq = cq_ref[...]
    ms = jnp.sum(cq * cq, axis=-1, keepdims=True) * (1.0 / A_Q_LORA)
    cqb = ((cq * lax.rsqrt(ms + LN_EPS)) * gq_ref[...]).astype(BF16)
    q = jnp.dot(cqb, wq_ref[...], preferred_element_type=F32)
    qr = jnp.dot(cqb, wqr_ref[...], preferred_element_type=F32)
    cos = jnp.concatenate([cosq_ref[...]] * A_HEADS, axis=1)
    sin = jnp.concatenate([sinq_ref[...]] * A_HEADS, axis=1)
    qa_ref[...] = ((q * cos + qr * sin) * (A_SCALE * LOG2E)).astype(BF16)

    ck = ck_ref[...]
    ckv = ck[:, :A_KV_LORA]
    ms2 = jnp.mean(ckv * ckv, axis=-1, keepdims=True)
    ckv_ref[...] = (ckv * lax.rsqrt(ms2 + LN_EPS)) * gkv_ref[...]
    kr = ck[:, A_KV_LORA:]
    krr = pltpu.roll(kr, LANES - A_ROPE, 1)
    kr_new = kr * cosk_ref[...] + krr * sink_ref[...]
    kr_ref[...] = kr_new
    kro_ref[...] = kr_new[:, :A_ROPE]


def _aprep(proj3, tabs, gq, gkv, wq, wqr, stacks, layer, tm):
    b, t, _ = proj3.shape
    cosq, sinq, cosk, sink = tabs
    tab_spec = pl.BlockSpec((tm, LANES), lambda bi, i: (i, 0))
    full = lambda shape: pl.BlockSpec(shape, lambda bi, i: (0,) * len(shape))
    blk = lambda cb: pl.BlockSpec((None, tm, 256), lambda bi, i: (bi, i, cb))
    rows = lambda w: pl.BlockSpec((None, tm, w), lambda bi, i: (bi, i, 0))
    heads = pl.BlockSpec((None, B_HEADS, tm, B_DIM), lambda bi, i: (bi, 0, i, 0))
    stacked = lambda a: pl.BlockSpec((None, None, tm, a.shape[-1]), lambda bi, i: (layer, bi, i, 0))
    untouched = pl.BlockSpec(memory_space=pl.ANY)
    n_in = 13
    out = pl.pallas_call(
        _aprep_kernel,
        out_shape=(jax.ShapeDtypeStruct((b, t, 512), BF16),
                   jax.ShapeDtypeStruct((b, t, LANES), F32),
                   jax.ShapeDtypeStruct((b, B_HEADS, t, B_DIM), BF16),
                   jax.ShapeDtypeStruct((b, B_HEADS, t, B_DIM), BF16),
                   jax.ShapeDtypeStruct((b, t, LANES), BF16))
        + tuple(jax.ShapeDtypeStruct(a.shape, a.dtype) for a in stacks),
        grid=(b, t // tm),
        in_specs=[blk(CB_CQ), blk(CB_CK), blk(CB_BK), blk(CB_BV), blk(CB_IDX),
                  tab_spec, tab_spec, tab_spec, tab_spec,
                  full((1, 256)), full((1, A_KV_LORA)), full((256, 512)), full((256, 512))]
        + [untouched] * len(stacks),
        out_specs=(rows(512), rows(LANES), heads, heads, rows(LANES))
        + tuple(stacked(a) for a in stacks),
        input_output_aliases={n_in + k: 5 + k for k in range(len(stacks))},
        compiler_params=_cparams(("parallel", "parallel")),
        name="mla_prep",
    )(proj3, proj3, proj3, proj3, proj3, cosq, sinq, cosk, sink, gq, gkv, wq, wqr, *stacks)
    return out[:5], tuple(out[5:])


def _kvup_kernel(ckv_ref, kr_ref, wkn_ref, wv_ref, e_ref, k_ref, v_ref):
    cb = ckv_ref[...].astype(BF16)
    k = jnp.dot(cb, wkn_ref[...], preferred_element_type=F32)
    k = k + jnp.dot(kr_ref[...].astype(BF16), e_ref[...], preferred_element_type=F32)
    k_ref[...] = k.astype(BF16)
    v = jnp.dot(cb, wv_ref[...], preferred_element_type=F32).astype(BF16)
    for h in range(A_HEADS):
        v_ref[h] = v[:, h * A_V:(h + 1) * A_V]


def _kvup(ckv2d, kr2d, wkn, wv, e, tm):
    m = ckv2d.shape[0]
    full = lambda shape: pl.BlockSpec(shape, lambda i: (0,) * len(shape))
    return pl.pallas_call(
        _kvup_kernel,
        out_shape=(jax.ShapeDtypeStruct((m, 512), BF16),
                   jax.ShapeDtypeStruct((A_HEADS, m, A_V), BF16)),
        grid=(m // tm,),
        in_specs=[pl.BlockSpec((tm, A_KV_LORA), lambda i: (i, 0)),
                  pl.BlockSpec((tm, LANES), lambda i: (i, 0)),
                  full((A_KV_LORA, 512)), full((A_KV_LORA, 256)), full((LANES, 512))],
        out_specs=(pl.BlockSpec((tm, 512), lambda i: (i, 0)),
                   pl.BlockSpec((A_HEADS, tm, A_V), lambda i: (0, i, 0))),
        compiler_params=_cparams(("parallel",)),
        name="kv_up",
    )(ckv2d, kr2d, wkn, wv, e)


def _online(carry, s, v):
    m, l, acc = carry
    m_new = jnp.maximum(m, jnp.max(s, axis=1, keepdims=True))
    alpha = jnp.exp2(m - m_new)
    p = jnp.exp2(s - m_new)
    l = alpha * l + jnp.sum(p, axis=1, keepdims=True)
    acc = alpha * acc + jnp.dot(p.astype(BF16), v, preferred_element_type=F32)
    return m_new, l, acc


def _softmax_init(tq, dv):
    return (jnp.full((tq, 1), M_INIT, F32), jnp.zeros((tq, 1), F32), jnp.zeros((tq, dv), F32))


def _visible_limit(qpos, s_valid):
    return jnp.minimum((jnp.right_shift(qpos, CHUNK_SHIFT) + 1) * CHUNK, s_valid)


def _mla_kernel(q_ref, k_ref, v_ref, o_ref, *, tq, tk, past, s_valid):
    i = pl.program_id(1)
    q0 = past + i * tq
    qpos = q0 + lax.broadcasted_iota(jnp.int32, (tq, 1), 0)
    q_lim = _visible_limit(qpos, s_valid)
    lim_first = _visible_limit(q0, s_valid)
    lim_last = _visible_limit(q0 + tq - 1, s_valid)
    n_full = lim_first // tk
    n_tot = (lim_last + tk - 1) // tk
    q = q_ref[...]
    qhs = [q[:, h * LANES:(h + 1) * LANES] for h in range(A_HEADS)]

    def body(j, carries, masked):
        ks = pl.ds(pl.multiple_of(j * tk, tk), tk)
        if masked:
            kpos = j * tk + lax.broadcasted_iota(jnp.int32, (1, tk), 1)
            vis = kpos < q_lim
        out = []
        for h in range(A_HEADS):
            s = lax.dot_general(qhs[h], k_ref[ks, h * LANES:(h + 1) * LANES], _NT,
                                preferred_element_type=F32)
            if masked:
                s = jnp.where(vis, s, NEG)
            out.append(_online(carries[h], s, v_ref[h, ks, :]))
        return tuple(out)

    init = tuple(_softmax_init(tq, A_V) for _ in range(A_HEADS))
    carries = lax.fori_loop(0, n_full, functools.partial(body, masked=False), init)
    carries = lax.fori_loop(n_full, n_tot, functools.partial(body, masked=True), carries)
    for h in range(A_HEADS):
        _, l, acc = carries[h]
        o_ref[:, h * A_V:(h + 1) * A_V] = acc / l


def _mla(qa, kx, v, tq, tk, past, s_valid):
    b, t, _ = qa.shape
    s_pad = kx.shape[1]
    kern = functools.partial(_mla_kernel, tq=tq, tk=tk, past=past, s_valid=s_valid)
    return pl.pallas_call(
        kern,
        out_shape=jax.ShapeDtypeStruct((b, t, 256), F32),
        grid=(b, t // tq),
        in_specs=[pl.BlockSpec((None, tq, 512), lambda bi, i: (bi, i, 0)),
                  pl.BlockSpec((None, s_pad, 512), lambda bi, i: (bi, 0, 0)),
                  pl.BlockSpec((A_HEADS, None, s_pad, A_V), lambda bi, i: (0, bi, 0, 0))],
        out_specs=pl.BlockSpec((None, tq, 256), lambda bi, i: (bi, i, 0)),
        compiler_params=_cparams(("parallel", "arbitrary")),
        name="mla_attn",
    )(qa, kx, v)


KEY_NONE = -2 ** 31
KEY_MAX = 2 ** 31 - 1


def _order_key(x):
    b = lax.bitcast_convert_type(x, jnp.int32)
    return b ^ (jnp.right_shift(b, 31) & KEY_MAX)


def _order_key_inv(k):
    return lax.bitcast_convert_type(k ^ (jnp.right_shift(k, 31) & KEY_MAX), F32)


def _dsa_kernel(bq_ref, idx_ref, bk_ref, bv_ref, ki_ref, bias_ref, o_ref, sc_ref, sct_ref, msk_ref,
                *, tq, tqs, tkc, past, s_valid, topk):
    i = pl.program_id(1)
    q0 = past + i * tq
    qpos = q0 + lax.broadcasted_iota(jnp.int32, (tq, 1), 0)
    q_lim = _visible_limit(qpos, s_valid)
    lim_last = _visible_limit(q0 + tq - 1, s_valid)
    nch = (lim_last + tkc - 1) // tkc
    ngrp = tkc // LANES
    kf = float(topk)

    def chunk(c):
        return pl.ds(pl.multiple_of(c * tkc, tkc), tkc)

    def key_pos(c):
        return c * tkc + lax.broadcasted_iota(jnp.int32, (1, tkc), 1)

    def key_pos_col(c):
        return (c * tkc + lax.broadcasted_iota(jnp.int32, (tkc, 1), 0)).astype(F32)

    idx = idx_ref[...]
    lane = lax.broadcasted_iota(jnp.int32, (1, LANES), 1)
    qi = idx[:, :LANES]
    qhs = [jnp.where((lane >= h * IDX_DIM) & (lane < (h + 1) * IDX_DIM), qi, 0.0).astype(BF16)
           for h in range(IDX_HEADS)]
    wi = idx[:, LANES + IDX_DIM:LANES + IDX_DIM + IDX_HEADS] * IDX_SCALE
    whs = [wi[:, h:h + 1] for h in range(IDX_HEADS)]

    def score_chunk(c, carry):
        ki = ki_ref[chunk(c), :]
        tot = None
        for h in range(IDX_HEADS):
            d = lax.dot_general(qhs[h], ki, _NT, preferred_element_type=F32)
            term = whs[h] * jnp.maximum(d, 0.0)
            tot = term if tot is None else tot + term
        sc = jnp.where(key_pos(c) < q_lim, _order_key(tot), KEY_NONE)
        sc_ref[:, chunk(c)] = sc
        if tqs > tq:
            sc = jnp.concatenate([sc, jnp.full((tqs - tq, tkc), KEY_NONE, jnp.int32)], axis=0)
        for g in range(ngrp):
            rows = pl.ds(pl.multiple_of(c * tkc + g * LANES, LANES), LANES)
            for rg in range(tqs // LANES):
                sct_ref[rg, rows, :] = sc[rg * LANES:(rg + 1) * LANES, g * LANES:(g + 1) * LANES].T
        return carry

    def keys_t(c):
        return jnp.concatenate([sct_ref[rg, chunk(c), :] for rg in range(tqs // LANES)], axis=1)

    lax.fori_loop(0, nch, score_chunk, 0)

    def fold8(x, op):
        pair = {jnp.sum: jnp.add, jnp.min: jnp.minimum, jnp.max: jnp.maximum}[op]
        parts = [x[8 * r:8 * (r + 1)] for r in range(tkc // 8)]
        while len(parts) > 1:
            odd = parts[len(parts) - 1:] if len(parts) % 2 else []
            parts = [pair(parts[r], parts[r + 1]) for r in range(0, len(parts) - 1, 2)] + odd
        return parts[0]

    def count_where(pred):
        def body(c, acc):
            m = jnp.where(pred(keys_t(c), c), 1.0, 0.0)
            return acc + fold8(m, jnp.sum)

        def body2(j, acc):
            return body(2 * j + 1, body(2 * j, acc))
        acc = lax.fori_loop(0, nch // 2, body2, jnp.zeros((8, tqs), F32))
        acc = lax.fori_loop(2 * (nch // 2), nch, body, acc)
        return jnp.sum(acc, axis=0, keepdims=True)

    def count_gt(t):
        return count_where(lambda x, c: x > t)

    def search():
        lane_q = lax.broadcasted_iota(jnp.int32, (1, tqs), 1)
        nvis = jnp.where(lane_q < tq, _visible_limit(q0 + lane_q, s_valid), 0).astype(F32)

        def mm_body(c, carry):
            mn, mx = carry
            x = keys_t(c)
            xm = jnp.where(x == KEY_NONE, KEY_MAX, x)
            return (jnp.minimum(mn, fold8(xm, jnp.min)), jnp.maximum(mx, fold8(x, jnp.max)))
        mn, mx = lax.fori_loop(0, nch, mm_body, (jnp.full((8, tqs), KEY_MAX, jnp.int32),
                                                 jnp.full((8, tqs), KEY_NONE, jnp.int32)))
        mn = jnp.min(mn, axis=0, keepdims=True)
        mx = jnp.max(mx, axis=0, keepdims=True)
        done0 = jnp.where(nvis <= kf, 1.0, 0.0)
        mn = jnp.where(nvis > 0.5, mn, 0)
        mx = jnp.where(nvis > 0.5, mx, 0)
        lo0 = mn - 1
        zeros = jnp.zeros((1, tqs), F32)
        izeros = jnp.zeros((1, tqs), jnp.int32)

        def snap(lo, hi, clo, chi, t, done, tie, vt):
            def body(c, carry):
                a, b = carry
                x = keys_t(c)
                xa = jnp.where(x > lo, x, KEY_MAX)
                xb = jnp.where(x <= hi, x, KEY_NONE)
                return (jnp.minimum(a, fold8(xa, jnp.min)), jnp.maximum(b, fold8(xb, jnp.max)))
            a, b = lax.fori_loop(0, nch, body, (jnp.full((8, tqs), KEY_MAX, jnp.int32),
                                                jnp.full((8, tqs), KEY_NONE, jnp.int32)))
            a = jnp.min(a, axis=0, keepdims=True)
            b = jnp.max(b, axis=0, keepdims=True)
            nd = done < 0.5
            open_ = nd & (a < b)
            b1 = jnp.where(open_, b - 1, hi)
            a0 = jnp.where(open_, a, lo)
            cb = count_gt(b1)
            ca = count_gt(a0)
            b_hit = open_ & (cb == kf)
            b_tie = open_ & (cb > kf)
            rest = open_ & (cb < kf)
            a_hit = rest & (ca == kf)
            a_tie = rest & (ca < kf)
            rest = rest & (ca > kf)
            new_tie = (nd & (a >= b)) | b_tie | a_tie
            t = jnp.where(b_hit, b1, jnp.where(a_hit, a0, t))
            vt = jnp.where(a_tie, a, jnp.where(new_tie, b, vt))
            tie = jnp.where(new_tie, 1.0, tie)
            done = jnp.where(new_tie | b_hit | a_hit, 1.0, done)
            lo = jnp.where(rest, a0, lo)
            clo = jnp.where(rest, ca, clo)
            hi = jnp.where(rest, b1, hi)
            chi = jnp.where(rest, cb, chi)
            return lo, hi, clo, chi, t, done, tie, vt

        def cond(s):
            return (s[1] > 0) & (s[0] < SEARCH_MAX_IT)

        def one_pass(it, s):
            lo, hi, clo, chi, t, done, tie, vt = s
            stuck = hi <= lo + 1
            lo_f, hi_f = _order_key_inv(lo), _order_key_inv(hi)
            itp = lo_f + (hi_f - lo_f) * ((clo - kf) / (clo - chi))
            cand = _order_key(jnp.where(it % 2 == 1, itp, 0.5 * lo_f + 0.5 * hi_f))
            imid = (lo & hi) + jnp.right_shift(lo ^ hi, 1)
            mid = jnp.where((cand > lo) & (cand < hi), cand, imid)
            c = count_gt(mid)
            nd = done < 0.5
            live = nd & jnp.logical_not(stuck)
            hit = live & (c == kf)
            stk = nd & stuck
            up = live & (c > kf)
            dn = live & (c < kf)
            t = jnp.where(hit, mid, t)
            vt = jnp.where(stk, hi, vt)
            tie = jnp.where(stk, 1.0, tie)
            lo = jnp.where(up, mid, lo)
            clo = jnp.where(up, c, clo)
            hi = jnp.where(dn, mid, hi)
            chi = jnp.where(dn, c, chi)
            done = jnp.where(hit | stk, 1.0, done)
            return lo, hi, clo, chi, t, done, tie, vt

        def body(s):
            it = s[0] + 2
            st = one_pass(it - 1, one_pass(it - 2, s[2:]))
            do_snap = (it >= SNAP_START) & ((it - SNAP_START) % SNAP_EVERY == 0)
            lo, hi, clo, chi, t, done, tie, vt = lax.cond(do_snap, snap, lambda *a: a, *st)
            active = (jnp.max(1.0 - done) > 0.5).astype(jnp.int32)
            return it, active, lo, hi, clo, chi, t, done, tie, vt

        active0 = (jnp.max(1.0 - done0) > 0.5).astype(jnp.int32)
        s = lax.while_loop(cond, body, (jnp.int32(0), active0, lo0, mx, nvis, zeros, lo0, done0,
                                        zeros, izeros))
        t, tie, vt = s[6], s[8], s[9]

        def tie_cut():
            t2 = jnp.where(tie > 0.5, vt, t)
            need = kf - count_gt(t2)

            def bs_body(_, carry):
                lo_c, hi_c = carry
                mid_c = jnp.floor((lo_c + hi_c) * 0.5)
                cnt = count_where(lambda x, c: (x == vt) & (key_pos_col(c) < mid_c))
                ok = cnt >= need
                return jnp.where(ok, lo_c, mid_c), jnp.where(ok, mid_c, hi_c)

            nbits = int(s_valid).bit_length()
            _, cut = lax.fori_loop(0, nbits, bs_body,
                                   (zeros, jnp.full((1, tqs), float(1 << nbits), F32)))
            return t2, jnp.where(tie > 0.5, cut, 0.0)

        any_tie = jnp.max(tie) > 0.5
        t, cut = lax.cond(any_tie, tie_cut, lambda: (t, zeros))
        return t, tie, vt, cut

    def no_search():
        z = jnp.zeros((1, tqs), F32)
        return (jnp.full((1, tqs), KEY_NONE, jnp.int32), z, jnp.zeros((1, tqs), jnp.int32), z)

    t, tie, vt, cut = lax.cond(lim_last > topk, search, no_search)

    def to_col(v):
        groups = [jnp.broadcast_to(v[:, rg * LANES:(rg + 1) * LANES], (LANES, LANES)).T
                  for rg in range(tqs // LANES)]
        full = groups[0] if len(groups) == 1 else jnp.concatenate(groups, axis=0)
        return full[:tq]

    t, vt, cut = to_col(t), to_col(vt), to_col(cut)
    tied = to_col(tie) > 0.5
    lane_f = lax.broadcasted_iota(jnp.int32, (1, LANES), 1).astype(F32)

    def to_mask(c, carry):
        for g in range(ngrp):
            cols = pl.ds(pl.multiple_of(c * tkc + g * LANES, LANES), LANES)
            x = sc_ref[:, cols]
            kpos = (c * tkc + g * LANES).astype(F32) + lane_f
            sel = (x > t) | (tied & (x == vt) & (kpos < cut))
            msk_ref[:, cols] = jnp.where(sel, 0.0, NEG)
        return carry

    lax.fori_loop(0, nch, to_mask, 0)

    bq = (bq_ref[...] * (B_SCALE * LOG2E)).astype(BF16)
    qbs = [bq[:, h * B_DIM:(h + 1) * B_DIM] for h in range(B_HEADS)]
    d0 = q0 // LANES

    def att_step(c, carries, nw):
        ks = pl.ds(pl.multiple_of(c * tkc, tkc), nw * tkc)
        msk = msk_ref[:, ks]
        rows = min(tq, LANES)
        out = []
        for h in range(B_HEADS):
            s = lax.dot_general(qbs[h], bk_ref[h, ks, :], _NT, preferred_element_type=F32)
            bias = jnp.concatenate(
                [jnp.concatenate([bias_ref[jnp.clip(d0 + rg - (c * ngrp + g), 0, 2), h, :rows, :]
                                  for g in range(nw * ngrp)], axis=1)
                 for rg in range(-(-tq // LANES))], axis=0)
            out.append(_online(carries[h], s + (msk + bias), bv_ref[h, ks, :]))
        return tuple(out)

    init = tuple(_softmax_init(tq, B_DIM) for _ in range(B_HEADS))
    nquad = nch // 4
    npair = (nch - 4 * nquad) // 2
    carries = lax.fori_loop(0, nquad, lambda j, cs: att_step(4 * j, cs, 4), init)
    carries = lax.fori_loop(0, npair, lambda j, cs: att_step(4 * nquad + 2 * j, cs, 2), carries)
    carries = lax.fori_loop(4 * nquad + 2 * npair, nch, lambda c, cs: att_step(c, cs, 1), carries)
    for h in range(B_HEADS):
        _, l, acc = carries[h]
        o_ref[:, h * B_DIM:(h + 1) * B_DIM] = acc / l


def _dsa(proj3, bk, bv, ki, bias, tq, tkc, past, s_valid, topk):
    b, t, _ = proj3.shape
    s_pad = bk.shape[2]
    tqs = -(-tq // LANES) * LANES
    assert past % LANES == 0 and (tq % LANES == 0 or t == tq) and s_pad % tkc == 0
    kern = functools.partial(_dsa_kernel, tq=tq, tqs=tqs, tkc=tkc, past=past, s_valid=s_valid,
                             topk=topk)
    return pl.pallas_call(
        kern,
        out_shape=jax.ShapeDtypeStruct((b, t, 256), F32),
        grid=(b, t // tq),
        in_specs=[pl.BlockSpec((None, tq, 256), lambda bi, i: (bi, i, CB_BQ)),
                  pl.BlockSpec((None, tq, 256), lambda bi, i: (bi, i, CB_IDX)),
                  pl.BlockSpec((None, B_HEADS, s_pad, B_DIM), lambda bi, i: (bi, 0, 0, 0)),
                  pl.BlockSpec((None, B_HEADS, s_pad, B_DIM), lambda bi, i: (bi, 0, 0, 0)),
                  pl.BlockSpec((None, s_pad, LANES), lambda bi, i: (bi, 0, 0)),
                  pl.BlockSpec((3, B_HEADS, LANES, LANES), lambda bi, i: (0, 0, 0, 0))],
        out_specs=pl.BlockSpec((None, tq, 256), lambda bi, i: (bi, i, 0)),
        scratch_shapes=[pltpu.VMEM((tq, s_pad), jnp.int32), pltpu.VMEM((tqs // LANES, s_pad, LANES), jnp.int32),
                        pltpu.VMEM((tq, s_pad), F32)],
        compiler_params=_cparams(("parallel", "arbitrary")),
        name="dsa_attn",
    )(proj3, proj3, bk, bv, ki, bias)


def _pool_kernel(u_ref, prev_ref, w_ref, scale_ref, o_ref, st_ref, ext_ref, *, tr, past):
    i = pl.program_id(1)

    @pl.when(i == 0)
    def _():
        ext_ref[1:16, :] = prev_ref[...]

    u = u_ref[...]
    ext_ref[16:16 + tr, :] = u

    def tap(k, lo):
        return ext_ref[16 - k:16 - k + tr, lo:lo + LANES]

    pos1 = past + i * tr + 1 + lax.broadcasted_iota(jnp.int32, (tr, 1), 0)
    cnt = [jnp.minimum(w, pos1).astype(F32) for w in POOL_WINDOWS]
    lane = lax.broadcasted_iota(jnp.int32, (1, LANES), 1)
    first = lane < POOL_GROUP

    s2 = tap(0, 0) + tap(1, 0)
    s4 = s2 + tap(2, 0) + tap(3, 0)
    s8 = tap(0, LANES)
    for k in range(1, 8):
        s8 = s8 + tap(k, LANES)
    s16 = s8
    for k in range(8, 16):
        s16 = s16 + tap(k, LANES)
    m_lo = jnp.where(first, s2 / cnt[0], s4 / cnt[1])
    m_hi = jnp.where(first, s8 / cnt[2], s16 / cnt[3])
    pooled = jnp.concatenate([m_lo, m_hi], axis=1) - u
    mixed = jnp.dot(pooled.astype(BF16), w_ref[...], preferred_element_type=F32)
    o_ref[...] = mixed * scale_ref[...]

    new_state = ext_ref[1 + tr:16 + tr, :]
    st_ref[...] = new_state
    ext_ref[1:16, :] = new_state


def _pool(proj3, prev, w_bd, scale, tr, past):
    b, t, _ = proj3.shape
    kern = functools.partial(_pool_kernel, tr=tr, past=past)
    return pl.pallas_call(
        kern,
        out_shape=(jax.ShapeDtypeStruct((b, t, POOL_WIDTH), F32),
                   jax.ShapeDtypeStruct((b, POOL_STATE, POOL_WIDTH), F32)),
        grid=(b, t // tr),
        in_specs=[pl.BlockSpec((None, tr, 256), lambda bi, i: (bi, i, CB_POOL)),
                  pl.BlockSpec((None, POOL_STATE, POOL_WIDTH), lambda bi, i: (bi, 0, 0)),
                  pl.BlockSpec((256, 256), lambda bi, i: (0, 0)),
                  pl.BlockSpec((1, 256), lambda bi, i: (0, 0))],
        out_specs=(pl.BlockSpec((None, tr, 256), lambda bi, i: (bi, i, 0)),
                   pl.BlockSpec((None, POOL_STATE, POOL_WIDTH), lambda bi, i: (bi, 0, 0))),
        scratch_shapes=[pltpu.VMEM((tr + 16, POOL_WIDTH), F32)],
        compiler_params=_cparams(("parallel", "arbitrary")),
        name="pool_mix",
    )(proj3, prev, w_bd, scale)


def _conv_kernel(u_ref, prev_ref, w_ref, b_ref, g_ref, beta_ref, o_ref, st_ref, ext_ref, sh_ref,
                 *, tr, sub):
    i = pl.program_id(1)
    npre = CONV_K - 1

    @pl.when(i == 0)
    def _():
        ext_ref[2:2 + npre, :] = prev_ref[...]

    u = u_ref[...]
    ext_ref[32:32 + tr, :] = u[:, :CONV_WIDTH] * _sigmoid(u[:, CONV_WIDTH:])

    for r in range(8):
        n = tr + 8 * ((CONV_K - 1 - r) // 8)
        sh_ref[r, 0:n, :] = ext_ref[2 + r:2 + r + n, :]

    for r0 in range(0, tr, sub):
        acc = None
        for k in range(CONV_K):
            a, r = divmod(k, 8)
            term = sh_ref[r, r0 + 8 * a:r0 + 8 * a + sub, :] * w_ref[k:k + 1, :]
            acc = term if acc is None else acc + term
        y = _ln_rows(acc + b_ref[...], g_ref[...], beta_ref[...])
        o_ref[r0:r0 + sub, :] = y * _sigmoid(y)

    new_state = ext_ref[2 + tr:2 + tr + npre, :]
    st_ref[...] = new_state
    ext_ref[2:2 + npre, :] = new_state


def _conv(proj3, prev, w, bias, g, beta, tr, sub):
    b, t, _ = proj3.shape
    npre = CONV_K - 1
    kern = functools.partial(_conv_kernel, tr=tr, sub=sub)
    vec = pl.BlockSpec((1, CONV_WIDTH), lambda bi, i: (0, 0))
    return pl.pallas_call(
        kern,
        out_shape=(jax.ShapeDtypeStruct((b, t, CONV_WIDTH), F32),
                   jax.ShapeDtypeStruct((b, npre, CONV_WIDTH), F32)),
        grid=(b, t // tr),
        in_specs=[pl.BlockSpec((None, tr, 512), lambda bi, i: (bi, i, CB_CONV)),
                  pl.BlockSpec((None, npre, CONV_WIDTH), lambda bi, i: (bi, 0, 0)),
                  pl.BlockSpec((CONV_K, CONV_WIDTH), lambda bi, i: (0, 0)),
                  vec, vec, vec],
        out_specs=(pl.BlockSpec((None, tr, CONV_WIDTH), lambda bi, i: (bi, i, 0)),
                   pl.BlockSpec((None, npre, CONV_WIDTH), lambda bi, i: (bi, 0, 0))),
        scratch_shapes=[pltpu.VMEM((tr + 32, CONV_WIDTH), F32),
                        pltpu.VMEM((8, tr + 24, CONV_WIDTH), F32)],
        compiler_params=_cparams(("parallel", "arbitrary")),
        name="conv_module",
    )(proj3, prev, w, bias, g, beta)


def _merge_kernel(a_ref, b_ref, c_ref, d_ref, x_ref, wg_ref, wb_ref, wo_ref, g_ref, beta_ref,
                  o_ref):
    x = x_ref[...]
    xb = x.astype(BF16)
    mixed = None
    for n, br_ref in enumerate((a_ref, b_ref, c_ref, d_ref)):
        gate = jnp.dot(xb, wg_ref[:, n * D_MODEL:(n + 1) * D_MODEL], preferred_element_type=F32)
        br = jnp.dot(br_ref[...].astype(BF16), wb_ref[n], preferred_element_type=F32)
        term = _sigmoid(gate) * br
        mixed = term if mixed is None else mixed + term
    y = jnp.dot(mixed.astype(BF16), wo_ref[...], preferred_element_type=F32)
    o_ref[...] = _ln_rows(ALPHA * x + y, g_ref[...], beta_ref[...])


def _merge(oa, ob, oc, od, x3, wg, wb, wo, g, beta, tm):
    b, t, _ = x3.shape
    br_spec = pl.BlockSpec((None, tm, 256), lambda bi, i: (bi, i, 0))
    const = lambda shape: pl.BlockSpec(shape, lambda bi, i: (0,) * len(shape),
                                       pipeline_mode=pl.Buffered(1))
    return pl.pallas_call(
        _merge_kernel,
        out_shape=jax.ShapeDtypeStruct((b, t, D_MODEL), F32),
        grid=(b, t // tm),
        in_specs=[br_spec, br_spec, br_spec, br_spec,
                  pl.BlockSpec((None, tm, D_MODEL), lambda bi, i: (bi, i, 0)),
                  const((D_MODEL, N_BRANCH * D_MODEL)), const((N_BRANCH, 256, D_MODEL)),
                  const((D_MODEL, D_MODEL)), const((1, D_MODEL)), const((1, D_MODEL))],
        out_specs=pl.BlockSpec((None, tm, D_MODEL), lambda bi, i: (bi, i, 0)),
        compiler_params=_cparams(("parallel", "parallel")),
        name="merge_ln1",
    )(oa, ob, oc, od, x3, wg, wb, wo, g, beta)


FF_TC = 256
FF_NC = D_FF // FF_TC


def _ffn_kernel(x_ref, st_ref, wup_ref, cw_ref, cb_ref, wd_ref, g_ref, beta_ref, o_ref, nst_ref,
                car_ref, ubuf_ref, h_ref, *, tm):
    @pl.when(pl.program_id(1) == 0)
    def _():
        car_ref[6:8, :] = st_ref[...]

    x = x_ref[...]
    xb = x.astype(BF16)

    def conv3(off, slot):
        cols = slice(off, off + FF_TC)
        u = jnp.dot(xb, wup_ref[:, cols], preferred_element_type=F32)
        ubuf_ref[slot, 6:8, :] = car_ref[6:8, cols]
        ubuf_ref[slot, 8:8 + tm, :] = u
        y = (cw_ref[0:1, cols] * ubuf_ref[slot, 6:6 + tm, :]
             + cw_ref[1:2, cols] * ubuf_ref[slot, 7:7 + tm, :]
             + cw_ref[2:3, cols] * u + cb_ref[:, cols])
        car_ref[6:8, cols] = ubuf_ref[slot, 6 + tm:8 + tm, :]
        return y

    for c in range(FF_NC):
        val = conv3(c * FF_TC, (2 * c) % 4)
        gate = conv3(D_FF + c * FF_TC, (2 * c + 1) % 4)
        h_ref[:, c * FF_TC:(c + 1) * FF_TC] = ((gate * _sigmoid(gate)) * val).astype(BF16)

    nst_ref[...] = car_ref[6:8, :]
    y = jnp.dot(h_ref[...], wd_ref[...], preferred_element_type=F32)
    o_ref[...] = _ln_rows(ALPHA * x + y, g_ref[...], beta_ref[...])


def _ffn(x3, st, w_up, cw, cb, w_dn, g, beta, tm):
    b, t, _ = x3.shape
    kern = functools.partial(_ffn_kernel, tm=tm)
    nk = FFN_K - 1
    const = lambda shape: pl.BlockSpec(shape, lambda bi, i: (0,) * len(shape),
                                       pipeline_mode=pl.Buffered(1))
    return pl.pallas_call(
        kern,
        out_shape=(jax.ShapeDtypeStruct((b, t, D_MODEL), F32),
                   jax.ShapeDtypeStruct((b, nk, 2 * D_FF), F32)),
        grid=(b, t // tm),
        in_specs=[pl.BlockSpec((None, tm, D_MODEL), lambda bi, i: (bi, i, 0)),
                  pl.BlockSpec((None, nk, 2 * D_FF), lambda bi, i: (bi, 0, 0)),
                  const((D_MODEL, 2 * D_FF)), const((FFN_K, 2 * D_FF)), const((1, 2 * D_FF)),
                  const((D_FF, D_MODEL)), const((1, D_MODEL)), const((1, D_MODEL))],
        out_specs=(pl.BlockSpec((None, tm, D_MODEL), lambda bi, i: (bi, i, 0)),
                   pl.BlockSpec((None, nk, 2 * D_FF), lambda bi, i: (bi, 0, 0))),
        scratch_shapes=[pltpu.VMEM((8, 2 * D_FF), F32),
                        pltpu.VMEM((4, tm + 8, FF_TC), F32),
                        pltpu.VMEM((tm, D_FF), BF16)],
        compiler_params=_cparams(("parallel", "arbitrary")),
        name="conv_ffn_ln2",
    )(x3, st, w_up, cw, cb, w_dn, g, beta)


def _rope_tables(past, t):
    half = A_ROPE // 2
    freqs = ROPE_BASE ** (-jnp.arange(half, dtype=F32) / half)
    pos = past + jnp.arange(t, dtype=jnp.int32)
    ang = pos.astype(F32)[:, None] * freqs[None, :]
    cos, sin = jnp.cos(ang), jnp.sin(ang)
    c2 = jnp.concatenate([cos, cos], axis=1)
    s2 = jnp.concatenate([sin, sin], axis=1)
    one = jnp.ones((t, A_NOPE), F32)
    z = lambda n: jnp.zeros((t, n), F32)
    cosq = jnp.concatenate([one, c2, z(LANES - A_NOPE - A_ROPE)], axis=1)
    sinq = jnp.concatenate([z(A_NOPE), s2, z(LANES - A_NOPE - A_ROPE)], axis=1)
    cosk = jnp.concatenate([c2, z(LANES - A_ROPE)], axis=1)
    sink = jnp.concatenate([s2, z(LANES - A_ROPE)], axis=1)
    return cosq, sinq, cosk, sink


_BUCKET_EDGES = (12, 16, 23, 32, 46, 64, 91)


def _rel_bucket(rel):
    nb = REL_BUCKETS // 2
    max_exact = nb // 2
    n = np.abs(rel)
    large = max_exact + sum((n >= e).astype(np.int64) for e in _BUCKET_EDGES)
    return np.where(rel > 0, nb, 0) + np.where(n < max_exact, n, large)


def _bias_tiles(rel_bias):
    r = np.arange(LANES)[:, None]
    c = np.arange(LANES)[None, :]
    buckets = np.stack([_rel_bucket(c - r - LANES * d) for d in range(3)])
    tiles = jnp.zeros((B_HEADS,) + buckets.shape, F32)
    for bkt in np.unique(buckets):
        sel = jnp.asarray(buckets == bkt)
        tiles = jnp.where(sel[None], rel_bias[int(bkt)][:, None, None, None], tiles)
    return jnp.transpose(tiles, (1, 0, 2, 3)) * LOG2E


def _pack_w_in(w):
    z = lambda n: jnp.zeros((D_MODEL, n), F32)
    kr = w[:, O_KROPE:O_KROPE + A_ROPE]
    half = A_ROPE // 2
    kr_rot = jnp.concatenate([-kr[:, half:], kr[:, :half]], axis=1)
    parts = [
        w[:, O_CONV:O_CONV + 2 * CONV_WIDTH],
        w[:, O_CQ:O_CQ + A_Q_LORA], z(256 - A_Q_LORA),
        w[:, O_CKV:O_CKV + A_KV_LORA], kr, kr_rot, z(256 - A_KV_LORA - 2 * A_ROPE),
        w[:, O_BQ:O_BQ + 256], w[:, O_BK:O_BK + 256], w[:, O_BV:O_BV + 256],
        w[:, O_QIDX:O_QIDX + 128], w[:, O_KIDX:O_KIDX + IDX_DIM], w[:, O_WIDX:O_WIDX + IDX_HEADS],
        z(256 - 128 - IDX_DIM - IDX_HEADS),
        w[:, O_POOL:O_POOL + POOL_WIDTH],
    ]
    return jnp.concatenate(parts, axis=1).astype(BF16)


def _pack_w_qup(w):
    zrow = lambda m: jnp.zeros((256 - A_Q_LORA, m.shape[1]), F32)
    half = A_ROPE // 2
    dq = A_NOPE + A_ROPE
    cols, rcols = [], []
    for h in range(A_HEADS):
        nope = w[:, h * dq:h * dq + A_NOPE]
        rp = w[:, h * dq + A_NOPE:(h + 1) * dq]
        pad = jnp.zeros((A_Q_LORA, LANES - dq), F32)
        cols += [nope, rp, pad]
        rcols += [jnp.zeros_like(nope), -rp[:, half:], rp[:, :half], pad]
    wq = jnp.concatenate(cols, axis=1)
    wqr = jnp.concatenate(rcols, axis=1)
    wq = jnp.concatenate([wq, zrow(wq)], axis=0).astype(BF16)
    wqr = jnp.concatenate([wqr, zrow(wqr)], axis=0).astype(BF16)
    return wq, wqr


def _pack_w_kvup(w):
    dk = A_NOPE + A_V
    kc, vc = [], []
    for h in range(A_HEADS):
        kc += [w[:, h * dk:h * dk + A_NOPE], jnp.zeros((A_KV_LORA, LANES - A_NOPE), F32)]
        vc += [w[:, h * dk + A_NOPE:(h + 1) * dk]]
    return jnp.concatenate(kc, axis=1).astype(BF16), jnp.concatenate(vc, axis=1).astype(BF16)


def _rope_placement():
    e = np.zeros((LANES, A_HEADS * LANES), np.float32)
    for h in range(A_HEADS):
        for r in range(A_ROPE):
            e[r, h * LANES + A_NOPE + r] = 1.0
    return jnp.asarray(e, dtype=BF16)


def _pool_blockdiag(pw):
    out = jnp.zeros((POOL_WIDTH, POOL_WIDTH), F32)
    for g in range(len(POOL_WINDOWS)):
        out = out.at[g * POOL_GROUP:(g + 1) * POOL_GROUP, g * POOL_GROUP:(g + 1) * POOL_GROUP].set(pw[g])
    return out.astype(BF16)


class _Tiles:
    def __init__(self, t):
        prompt = t > 128
        self.rows = 1024 if prompt else 128
        self.in_tn = 2304
        self.prep = 1024 if prompt else t
        self.kv = 2048 if prompt else 17 * LANES
        self.mla_q = 1024 if prompt else t
        self.mla_k = 1024 if prompt else 17 * LANES
        self.dsa_q = 256 if prompt else t
        self.dsa_kc = 512 if prompt else 17 * LANES
        self.pool = 1024 if prompt else t
        self.conv = 512 if prompt else t
        self.conv_sub = 64 if prompt else t
        self.merge = 512 if prompt else t
        self.ffn = 512 if prompt else t
        self.s_align = 512 if prompt else 17 * LANES


def _trunk_layer(x3, caches, lw, consts, past, stacks, layer):
    (w_in, w_gates, gq, gkv, wq, wqr, wkn, wv, pool_bd, pool_scale, conv_w, conv_b, conv_g, conv_beta,
     w_branch, w_out, ln1_g, ln1_b, w_up, ffn_cw, ffn_cb, w_down, ln2_g, ln2_b) = lw
    tabs, bias, e_place = consts
    b, t, _ = x3.shape
    tl = _Tiles(t)
    s_valid = past + t
    s_pad = -(-s_valid // tl.s_align) * tl.s_align
    topk = min(TOPK_MAX, s_valid // 4)

    proj = _inproj(x3.reshape(b * t, D_MODEL), w_in, tl.rows, tl.in_tn)
    proj3 = proj.reshape(b, t, PROJ_W)
    (qa, kr128, bk_hm, bv_hm, ki_rep), stacks = _aprep(
        proj3, tabs, gq, gkv, wq, wqr, stacks, layer, tl.prep)
    ckv_n = stacks[0][layer]

    def keys(cache, new, width):
        parts = [new] if cache is None else [cache.reshape(b, past, -1), new]
        if s_pad > s_valid:
            parts.append(jnp.zeros((b, s_pad - s_valid, width), new.dtype))
        return parts[0] if len(parts) == 1 else jnp.concatenate(parts, axis=1)

    c_ckv, c_krope, c_bk, c_bv, c_kidx, st_pool, st_conv, st_ffn = caches
    ckv_all = keys(c_ckv, ckv_n, A_KV_LORA)
    if c_krope is None:
        kr_all = keys(None, kr128, LANES)
    else:
        c_kr128 = jnp.pad(c_krope, ((0, 0), (0, 0), (0, LANES - A_ROPE)))
        kr_all = keys(c_kr128, kr128, LANES)
    if c_bk is None and s_pad == s_valid:
        bk_all, bv_all, ki_all = bk_hm, bv_hm, ki_rep
    else:
        head_major = lambda a: jnp.transpose(a.astype(BF16).reshape(b, s_pad, B_HEADS, B_DIM),
                                             (0, 2, 1, 3))
        bk_all = head_major(keys(c_bk, stacks[2][layer], 256))
        bv_all = head_major(keys(c_bv, stacks[3][layer], 256))
        ki_all = jnp.tile(keys(c_kidx, stacks[4][layer], IDX_DIM).astype(BF16), (1, 1, IDX_HEADS))

    kx, vx = _kvup(ckv_all.reshape(b * s_pad, A_KV_LORA), kr_all.reshape(b * s_pad, LANES),
                   wkn, wv, e_place, tl.kv)
    out_a = _mla(qa, kx.reshape(b, s_pad, 512), vx.reshape(A_HEADS, b, s_pad, A_V),
                 tl.mla_q, tl.mla_k, past, s_valid)
    out_b = _dsa(proj3, bk_all, bv_all, ki_all, bias, tl.dsa_q, tl.dsa_kc, past, s_valid, topk)
    out_c, new_pool = _pool(proj3, st_pool, pool_bd, pool_scale, tl.pool, past)
    out_d, new_conv = _conv(proj3, st_conv, conv_w, conv_b, conv_g, conv_beta, tl.conv, tl.conv_sub)
    x1 = _merge(out_a, out_b, out_c, out_d, x3, w_gates, w_branch, w_out, ln1_g, ln1_b, tl.merge)
    x2, new_ffn = _ffn(x1, st_ffn, w_up, ffn_cw, ffn_cb, w_down, ln2_g, ln2_b, tl.ffn)
    return x2, stacks, (new_pool, new_conv, new_ffn)


def kernel(x_prompt, x_sample, cache_a_ckv, cache_a_krope, cache_b_k, cache_b_v, cache_b_kidx,
           state_pool, state_conv, state_ffn, rel_bias, ln_in_g, ln_in_b, w_in, a_q_norm, a_kv_norm,
           a_w_qup, a_w_kvup, pool_w, pool_scale, conv_w, conv_b, conv_ln_g, conv_ln_b, w_branch,
           w_out, ln1_g, ln1_b, w_up, ffn_conv_w, ffn_conv_b, w_down, ln2_g, ln2_b):
    bp, tp, _ = x_prompt.shape
    bs, ts, _ = x_sample.shape
    past = cache_a_ckv.shape[2]
    depth = w_in.shape[0]

    xp = _layer_norm(x_prompt.reshape(bp * tp, D_MODEL), ln_in_g, ln_in_b,
                     _Tiles(tp).rows).reshape(bp, tp, D_MODEL)
    xs = _layer_norm(x_sample.reshape(bs * ts, D_MODEL), ln_in_g, ln_in_b,
                     _Tiles(ts).rows).reshape(bs, ts, D_MODEL)

    e_place = _rope_placement()
    bias = _bias_tiles(rel_bias)
    consts_p = (_rope_tables(0, tp), bias, e_place)
    consts_s = (_rope_tables(past, ts), bias, e_place)
    zeros_p = (None, None, None, None, None,
               jnp.zeros((bp, POOL_STATE, POOL_WIDTH), F32),
               jnp.zeros((bp, CONV_K - 1, CONV_WIDTH), F32),
               jnp.zeros((bp, FFN_K - 1, 2 * D_FF), F32))

    def state_stacks(b, t):
        return tuple(jnp.zeros((depth, b, t, w), F32)
                     for w in (A_KV_LORA, A_ROPE, 256, 256, IDX_DIM))

    stacks_p, stacks_s = state_stacks(bp, tp), state_stacks(bs, ts)
    row = lambda v: v.reshape(1, -1)
    p_states, s_states = [], []
    for l in range(depth):
        wq, wqr = _pack_w_qup(a_w_qup[l])
        wkn, wv = _pack_w_kvup(a_w_kvup[l])
        gq = jnp.concatenate([a_q_norm[l], jnp.zeros((256 - A_Q_LORA,), F32)]).reshape(1, 256)
        w_gates = w_in[l][:, O_GATES:O_GATES + N_BRANCH * D_MODEL].astype(BF16)
        lw = (_pack_w_in(w_in[l]), w_gates, gq, row(a_kv_norm[l]), wq, wqr, wkn, wv,
              _pool_blockdiag(pool_w[l]), row(pool_scale[l]), conv_w[l], row(conv_b[l]),
              row(conv_ln_g[l]), row(conv_ln_b[l]), w_branch[l].astype(BF16), w_out[l].astype(BF16),
              row(ln1_g[l]), row(ln1_b[l]), w_up[l].astype(BF16), ffn_conv_w[l], row(ffn_conv_b[l]),
              w_down[l].astype(BF16), row(ln2_g[l]), row(ln2_b[l]))
        xp, stacks_p, st_p = _trunk_layer(xp, zeros_p, lw, consts_p, 0, stacks_p, l)
        caches_s = (cache_a_ckv[l], cache_a_krope[l], cache_b_k[l], cache_b_v[l], cache_b_kidx[l],
                    state_pool[l], state_conv[l], state_ffn[l])
        xs, stacks_s, st_s = _trunk_layer(xs, caches_s, lw, consts_s, past, stacks_s, l)
        p_states.append(st_p)
        s_states.append(st_s)

    def outputs(stacks, small, b, t):
        ckv, krope, bk, bv, kidx = stacks
        heads = lambda a: a.reshape(depth, b, t, B_HEADS, B_DIM)
        return [ckv, krope, heads(bk), heads(bv), kidx] + [jnp.stack(a) for a in zip(*small)]

    return (xp, xs, *outputs(stacks_p, p_states, bp, tp), *outputs(stacks_s, s_states, bs, ts))
```
